```python
import jax, jax.numpy as jnp
from jax import lax
import numpy as np

D_MODEL = 1024
BATCH = 16
SEQ = 2048
DEPTH = 2

BRANCH_W = 512
N_BRANCH = 3
GM_GROUPS = 4
GM_GROUP_W = BRANCH_W // GM_GROUPS
GM_CHUNK = 128
RET_HEADS = 4
RET_HD = BRANCH_W // RET_HEADS
RET_CHUNK = 128
ROPE_BASE = 10000.0
GDN_HEADS = 4
GDN_HD = BRANCH_W // GDN_HEADS
GDN_CHUNK = 64
CONV_K = 3
EPS = 1e-6

IN_SPLITS = (BRANCH_W,) * 3 + (BRANCH_W,) * 4 + (BRANCH_W,) * 4 + (GDN_HEADS,) * 4
IN_COLS = sum(IN_SPLITS)
IN_OFFSETS = tuple(int(o) for o in np.cumsum(IN_SPLITS)[:-1])

kernel_name = "hybrid_gmlp_retnet_gdn_encoder"

F32 = jnp.float32


def rms_norm(x, g):
    xf = x.astype(F32)
    y = xf * lax.rsqrt(jnp.mean(xf * xf, axis=-1, keepdims=True) + EPS)
    return (y * g.astype(F32)).astype(x.dtype)


def layer_norm(x, g, b):
    xf = x.astype(F32)
    mu = jnp.mean(xf, axis=-1, keepdims=True)
    var = jnp.mean(jnp.square(xf - mu), axis=-1, keepdims=True)
    y = (xf - mu) * lax.rsqrt(var + EPS)
    return (y * g.astype(F32) + b.astype(F32)).astype(x.dtype)


def to_chunks(x, c):
    b, s, h, d = x.shape
    return x.reshape(b, s // c, c, h, d).transpose(1, 0, 3, 2, 4)


def from_chunks(x):
    n, b, h, c, d = x.shape
    return x.transpose(1, 0, 3, 2, 4).reshape(b, n * c, h, d)


def gmlp_branch(u, v, z, ln_g, ln_b, w_s, b_s):
    bsz, seq, _ = u.shape
    u = jax.nn.gelu(u, approximate=False)
    v = layer_norm(jax.nn.gelu(v, approximate=False), ln_g, ln_b)
    vc = v.reshape(bsz, seq // GM_CHUNK, GM_CHUNK, GM_GROUPS, GM_GROUP_W)
    mixed = jnp.einsum("gpq,bnqgc->bnpgc", w_s, vc) + b_s.T[None, None, :, :, None]
    return u * mixed.reshape(bsz, seq, BRANCH_W) * jax.nn.silu(z)


def rotary(x):
    seq, hd = x.shape[1], x.shape[-1]
    inv = ROPE_BASE ** (-jnp.arange(0, hd, 2, dtype=F32) / hd)
    ang = jnp.arange(seq, dtype=F32)[:, None] * inv[None, :]
    cos = jnp.cos(ang)[None, :, None, :]
    sin = jnp.sin(ang)[None, :, None, :]
    x1, x2 = x[..., : hd // 2], x[..., hd // 2:]
    return jnp.concatenate([x1 * cos - x2 * sin, x2 * cos + x1 * sin], axis=-1)


def retention_branch(q, k, v, z, decay_logit, norm_g):
    bsz, seq, _ = q.shape
    shp = (bsz, seq, RET_HEADS, RET_HD)
    q = rotary(q.reshape(shp).astype(F32))
    k = rotary(k.reshape(shp).astype(F32)) * (RET_HD ** -0.5)
    v = v.reshape(shp).astype(F32)
    log_g = jax.nn.log_sigmoid(decay_logit.astype(F32))
    lf = log_g[0][:, None]
    lb = log_g[1][:, None]
    pos = jnp.arange(RET_CHUNK, dtype=F32)[None, :]
    rel = pos.T - pos
    mask = jnp.exp(jnp.where(rel[None] >= 0, lf[:, :, None] * rel[None], -lb[:, :, None] * rel[None]))
    q_dec_f = jnp.exp(lf * (pos + 1.0))[None, :, :, None]
    k_dec_f = jnp.exp(lf * (RET_CHUNK - 1.0 - pos))[None, :, :, None]
    q_dec_b = jnp.exp(lb * (RET_CHUNK - pos))[None, :, :, None]
    k_dec_b = jnp.exp(lb * pos)[None, :, :, None]
    chunk_f = jnp.exp(lf * RET_CHUNK)[None, :, :, None]
    chunk_b = jnp.exp(lb * RET_CHUNK)[None, :, :, None]
    qc, kc, vc = to_chunks(q, RET_CHUNK), to_chunks(k, RET_CHUNK), to_chunks(v, RET_CHUNK)
    state0 = jnp.zeros((bsz, RET_HEADS, RET_HD, RET_HD), F32)

    def fwd_step(state, inp):
        qi, ki, vi = inp
        scores = jnp.einsum("bhqd,bhkd->bhqk", qi, ki) * mask[None]
        out = (jnp.einsum("bhqk,bhke->bhqe", scores, vi)
               + jnp.einsum("bhcd,bhde->bhce", qi * q_dec_f, state))
        state = state * chunk_f + jnp.einsum("bhcd,bhce->bhde", ki * k_dec_f, vi)
        return state, out

    def bwd_step(state, inp):
        qi, ki, vi = inp
        out = jnp.einsum("bhcd,bhde->bhce", qi * q_dec_b, state)
        state = state * chunk_b + jnp.einsum("bhcd,bhce->bhde", ki * k_dec_b, vi)
        return state, out

    _, o_f = lax.scan(fwd_step, state0, (qc, kc, vc))
    _, o_b = lax.scan(bwd_step, state0, (qc, kc, vc), reverse=True)
    o = from_chunks(o_f + o_b)
    mu = jnp.mean(o, axis=-1, keepdims=True)
    o = (o - mu) * lax.rsqrt(jnp.mean(jnp.square(o - mu), axis=-1, keepdims=True) + EPS)
    o = o.reshape(bsz, seq, BRANCH_W) * norm_g.astype(F32)
    return o.astype(z.dtype) * jax.nn.silu(z)


def centred_depthwise_conv(x, w):
    ch = x.shape[-1]
    pad = (CONV_K - 1) // 2
    return lax.conv_general_dilated(
        x, w[:, None, :].astype(x.dtype), window_strides=(1,), padding=[(pad, pad)],
        dimension_numbers=("NWC", "WIO", "NWC"), feature_group_count=ch)


def l2_normalize(x):
    return x * lax.rsqrt(jnp.sum(x * x, axis=-1, keepdims=True) + EPS)


def gdn_direction(q, k, v, beta, g):
    bsz = q.shape[0]
    c = GDN_CHUNK
    qc, kc, vc = to_chunks(q, c), to_chunks(k, c), to_chunks(v, c)
    bc = to_chunks(beta[..., None], c)[..., 0]
    gc = jnp.cumsum(to_chunks(g[..., None], c)[..., 0], axis=-1)
    idx = jnp.arange(c)
    incl = idx[:, None] >= idx[None, :]
    strict = idx[:, None] > idx[None, :]
    decay = jnp.exp(jnp.where(incl, gc[..., :, None] - gc[..., None, :], -jnp.inf))
    kb = kc * bc[..., None]
    lower = jnp.where(strict, jnp.einsum("nbhid,nbhjd->nbhij", kb, kc) * decay, 0.0)
    eye = jnp.eye(c, dtype=F32)
    rhs = jnp.concatenate([vc * bc[..., None], kb * jnp.exp(gc)[..., None]], axis=-1)
    sol = lax.linalg.triangular_solve(lower + eye, rhs, left_side=True, lower=True)
    u, w = sol[..., :GDN_HD], sol[..., GDN_HD:]

    def step(state, inp):
        qi, ki, ui, wi, gi, di = inp
        v_new = ui - jnp.einsum("bhcd,bhde->bhce", wi, state)
        scores = jnp.einsum("bhqd,bhkd->bhqk", qi, ki) * di
        out = (jnp.einsum("bhcd,bhde->bhce", qi * jnp.exp(gi)[..., None], state)
               + jnp.einsum("bhqk,bhke->bhqe", scores, v_new))
        g_last = gi[..., -1:]
        state = (state * jnp.exp(g_last)[..., None]
                 + jnp.einsum("bhcd,bhce->bhde", ki * jnp.exp(g_last - gi)[..., None], v_new))
        return state, out

    state0 = jnp.zeros((bsz, GDN_HEADS, GDN_HD, GDN_HD), F32)
    _, o = lax.scan(step, state0, (qc, kc, u, w, gc, decay))
    return from_chunks(o)


def gdn_branch(q, k, v, z, a_f, b_f, a_b, b_b, conv_w, a_log, dt_bias, norm_g):
    bsz, seq, _ = q.shape
    qkv = jax.nn.silu(centred_depthwise_conv(jnp.concatenate([q, k, v], axis=-1), conv_w))
    q, k, v = jnp.split(qkv.astype(F32), 3, axis=-1)
    shp = (bsz, seq, GDN_HEADS, GDN_HD)
    q = l2_normalize(q.reshape(shp)) * (GDN_HD ** -0.5)
    k = l2_normalize(k.reshape(shp))
    v = v.reshape(shp)
    a_rate = jnp.exp(a_log.astype(F32))
    dtb = dt_bias.astype(F32)
    g_f = -a_rate[0] * jax.nn.softplus(a_f.astype(F32) + dtb[0])
    g_b = -a_rate[1] * jax.nn.softplus(a_b.astype(F32) + dtb[1])
    beta_f = jax.nn.sigmoid(b_f.astype(F32))
    beta_b = jax.nn.sigmoid(b_b.astype(F32))
    o_f = gdn_direction(q, k, v, beta_f, g_f)
    rev = lambda t: jnp.flip(t, axis=1)
    o_b = rev(gdn_direction(rev(q), rev(k), rev(v), rev(beta_b), rev(g_b)))
    o = o_f + o_b
    o = o * lax.rsqrt(jnp.mean(o * o, axis=-1, keepdims=True) + EPS) * norm_g.astype(F32)
    return o.reshape(bsz, seq, BRANCH_W).astype(z.dtype) * jax.nn.silu(z)


def setup_inputs(seed: int = 0) -> dict:
    key = jax.random.key(seed)
    ks = jax.random.split(key, 17)
    nrm = lambda k, shape, scale: jax.random.normal(k, shape, F32) * scale
    x = nrm(ks[0], (BATCH, SEQ, D_MODEL), 1.0)
    norm_g = 1.0 + nrm(ks[1], (DEPTH, D_MODEL), 0.05)
    w_in = nrm(ks[2], (DEPTH, D_MODEL, IN_COLS), D_MODEL ** -0.5)
    gm_ln_g = 1.0 + nrm(ks[3], (DEPTH, BRANCH_W), 0.05)
    gm_ln_b = nrm(ks[4], (DEPTH, BRANCH_W), 0.02)
    gm_w_s = nrm(ks[5], (DEPTH, GM_GROUPS, GM_CHUNK, GM_CHUNK), GM_CHUNK ** -0.5)
    gm_b_s = 1.0 + nrm(ks[6], (DEPTH, GM_GROUPS, GM_CHUNK), 0.05)
    base_gamma = 1.0 - 2.0 ** (-5.0 - np.arange(RET_HEADS))
    base_logit = np.log(base_gamma / (1.0 - base_gamma)).astype(np.float32)
    ret_decay_logit = jnp.asarray(base_logit)[None, None, :] + nrm(ks[7], (DEPTH, 2, RET_HEADS), 0.1)
    ret_norm_g = 1.0 + nrm(ks[8], (DEPTH, BRANCH_W), 0.05)
    gdn_conv_w = nrm(ks[9], (DEPTH, CONV_K, 3 * BRANCH_W), CONV_K ** -0.5)
    gdn_a_log = jnp.log(jax.random.uniform(ks[10], (DEPTH, 2, GDN_HEADS), F32, 1.0, 16.0))
    dt = jnp.exp(jax.random.uniform(ks[11], (DEPTH, 2, GDN_HEADS), F32,
                                    float(np.log(1e-3)), float(np.log(1e-1))))
    gdn_dt_bias = dt + jnp.log(-jnp.expm1(-dt))
    gdn_norm_g = 1.0 + nrm(ks[12], (DEPTH, GDN_HD), 0.05)
    w_gate = nrm(ks[13], (DEPTH, D_MODEL, N_BRANCH, D_MODEL), D_MODEL ** -0.5)
    w_branch_out = nrm(ks[14], (DEPTH, N_BRANCH, BRANCH_W, D_MODEL), BRANCH_W ** -0.5)
    w_out = nrm(ks[15], (DEPTH, D_MODEL, D_MODEL), D_MODEL ** -0.5)
    final_norm_g = 1.0 + nrm(ks[16], (D_MODEL,), 0.05)
    return {"x": x, "norm_g": norm_g, "w_in": w_in, "gm_ln_g": gm_ln_g, "gm_ln_b": gm_ln_b,
            "gm_w_s": gm_w_s, "gm_b_s": gm_b_s, "ret_decay_logit": ret_decay_logit,
            "ret_norm_g": ret_norm_g, "gdn_conv_w": gdn_conv_w, "gdn_a_log": gdn_a_log,
            "gdn_dt_bias": gdn_dt_bias, "gdn_norm_g": gdn_norm_g, "w_gate": w_gate,
            "w_branch_out": w_branch_out, "w_out": w_out, "final_norm_g": final_norm_g}


def reference(x, norm_g, w_in, gm_ln_g, gm_ln_b, gm_w_s, gm_b_s, ret_decay_logit, ret_norm_g,
              gdn_conv_w, gdn_a_log, gdn_dt_bias, gdn_norm_g, w_gate, w_branch_out, w_out,
              final_norm_g):
    for l in range(DEPTH):
        h = rms_norm(x, norm_g[l])
        (gu, gv, gz, rq, rk, rv, rz, dq, dk, dv, dz,
         a_f, b_f, a_b, b_b) = jnp.split(h @ w_in[l], IN_OFFSETS, axis=-1)
        y_a = gmlp_branch(gu, gv, gz, gm_ln_g[l], gm_ln_b[l], gm_w_s[l], gm_b_s[l])
        y_b = retention_branch(rq, rk, rv, rz, ret_decay_logit[l], ret_norm_g[l])
        y_c = gdn_branch(dq, dk, dv, dz, a_f, b_f, a_b, b_b, gdn_conv_w[l], gdn_a_log[l],
                         gdn_dt_bias[l], gdn_norm_g[l])
        branches = jnp.stack([y_a, y_b, y_c], axis=2)
        proj = jnp.einsum("bsnw,nwd->bsnd", branches, w_branch_out[l])
        gates = jax.nn.sigmoid(jnp.einsum("bsd,dne->bsne", h, w_gate[l]))
        merged = jnp.sum(gates * proj, axis=2)
        x = x + merged @ w_out[l]
    return rms_norm(x, final_norm_g)
```

```python
import functools

import jax
import jax.numpy as jnp
from jax import lax
from jax.experimental import pallas as pl
from jax.experimental.pallas import tpu as pltpu

F32 = jnp.float32
BF16 = jnp.bfloat16

D_MODEL = 1024
BATCH = 16
SEQ = 2048
DEPTH = 2
TOKENS = BATCH * SEQ
BRANCH_W = 512
N_BRANCH = 3
GM_GROUPS = 4
GM_CHUNK = 128
HEADS = 4
HEAD_D = 128
RET_CHUNK = 128
GDN_CHUNK = 64
ROPE_BASE = 10000.0
EPS = 1e-6
LANES = 128
MAIN_COLS = 11 * BRANCH_W
COL_GU, COL_GV, COL_GZ = 0, 1, 2
COL_RQ, COL_RK, COL_RV, COL_RZ = 3, 4, 5, 6
COL_DQ, COL_DK, COL_DV, COL_DZ = 7, 8, 9, 10

VMEM_LIMIT = 56 * 1024 * 1024


def _dot(a, b, precision=None):
    return jnp.dot(a, b, preferred_element_type=F32, precision=precision)


def _dot_nt(a, b):
    return lax.dot_general(a, b, (((1,), (1,)), ((), ())), preferred_element_type=F32)


def _dot_tn(a, b):
    return lax.dot_general(a, b, (((0,), (0,)), ((), ())), preferred_element_type=F32)


def _silu(x):
    return x * jax.nn.sigmoid(x)


def _gelu(x):
    return 0.5 * x * (1.0 + lax.erf(x * 0.7071067811865476))


def _params(*sem):
    return pltpu.CompilerParams(dimension_semantics=sem, vmem_limit_bytes=VMEM_LIMIT)


def _inproj_body(x_ref, g_ref, w_ref, wab_ref, o_ref, ab_ref, h_ref):
    @pl.when(pl.program_id(1) == 0)
    def _():
        x = x_ref[...]
        h = x * lax.rsqrt(jnp.mean(x * x, axis=-1, keepdims=True) + EPS) * g_ref[...]
        hb = h.astype(BF16)
        h_ref[...] = hb
        ab_ref[...] = _dot(hb, wab_ref[...])

    o_ref[...] = _dot(h_ref[...], w_ref[...]).astype(o_ref.dtype)


def _in_proj(x, g, w_main, w_ab, tm=1024, tn=1408):
    grid = (TOKENS // tm, MAIN_COLS // tn)
    return pl.pallas_call(
        _inproj_body,
        grid=grid,
        in_specs=[
            pl.BlockSpec((tm, D_MODEL), lambda i, j: (i, 0)),
            pl.BlockSpec((1, D_MODEL), lambda i, j: (0, 0)),
            pl.BlockSpec((D_MODEL, tn), lambda i, j: (0, j)),
            pl.BlockSpec((D_MODEL, LANES), lambda i, j: (0, 0)),
        ],
        out_specs=[
            pl.BlockSpec((tm, tn), lambda i, j: (i, j)),
            pl.BlockSpec((tm, LANES), lambda i, j: (i, 0)),
        ],
        out_shape=[
            jax.ShapeDtypeStruct((TOKENS, MAIN_COLS), BF16),
            jax.ShapeDtypeStruct((TOKENS, LANES), F32),
        ],
        scratch_shapes=[pltpu.VMEM((tm, D_MODEL), BF16)],
        compiler_params=_params("arbitrary", "arbitrary"),
        name="in_proj",
    )(x, g, w_main, w_ab)


def _gmlp_body(u_ref, v_ref, z_ref, lng_ref, lnb_ref, ws_ref, bias_ref, o_ref):
    rows = u_ref.shape[0]
    for c in range(rows // GM_CHUNK):
        r = slice(c * GM_CHUNK, (c + 1) * GM_CHUNK)
        u = _gelu(u_ref[r, :].astype(F32))
        v = _gelu(v_ref[r, :].astype(F32))
        z = z_ref[r, :].astype(F32)
        mu = jnp.mean(v, axis=-1, keepdims=True)
        vc = v - mu
        var = jnp.mean(vc * vc, axis=-1, keepdims=True)
        vn = (vc * lax.rsqrt(var + EPS) * lng_ref[...] + lnb_ref[...]).astype(BF16)
        gate = u * _silu(z)
        for g in range(GM_GROUPS):
            cs = slice(g * LANES, (g + 1) * LANES)
            mixed = _dot(ws_ref[g], vn[:, cs]) + bias_ref[:, cs]
            o_ref[r, cs] = (gate[:, cs] * mixed).astype(o_ref.dtype)


def _gmlp(proj, ln_g, ln_b, w_s, bias, tm=512):
    col = lambda c: pl.BlockSpec((tm, BRANCH_W), lambda i, c=c: (i, c))
    full = lambda shape: pl.BlockSpec(shape, lambda i: (0,) * len(shape))
    return pl.pallas_call(
        _gmlp_body,
        grid=(TOKENS // tm,),
        in_specs=[col(COL_GU), col(COL_GV), col(COL_GZ), full((1, BRANCH_W)), full((1, BRANCH_W)),
                  full((GM_GROUPS, GM_CHUNK, GM_CHUNK)), full((GM_CHUNK, BRANCH_W))],
        out_specs=pl.BlockSpec((tm, BRANCH_W), lambda i: (i, 0)),
        out_shape=jax.ShapeDtypeStruct((TOKENS, BRANCH_W), BF16),
        compiler_params=_params("arbitrary"),
        name="gmlp",
    )(proj, proj, proj, ln_g, ln_b, w_s, bias)


def _ret_body(q_ref, k_ref, v_ref, z_ref, cos_ref, sin_ref, dl_ref, ng_ref, y_ref,
              qr_s, kr_s, of_s, ob_s, st_s):
    C = RET_CHUNK
    n_chunks = SEQ // C
    row = lax.broadcasted_iota(jnp.int32, (C, C), 0).astype(F32)
    colm = lax.broadcasted_iota(jnp.int32, (C, C), 1).astype(F32)
    rel = row - colm

    cos = cos_ref[...]
    sin = sin_ref[...]
    for h in range(HEADS):
        hs = slice(h * HEAD_D, (h + 1) * HEAD_D)
        q = q_ref[:, hs].astype(F32)
        k = k_ref[:, hs].astype(F32)
        qr_s[:, hs] = q * cos + pltpu.roll(q, HEAD_D // 2, 1) * sin
        kr_s[:, hs] = (k * cos + pltpu.roll(k, HEAD_D // 2, 1) * sin) * (HEAD_D ** -0.5)

    decs = []
    for h in range(HEADS):
        lf = jnp.broadcast_to(jax.nn.log_sigmoid(dl_ref[0, h])[0:1, :], (C, C))
        lb = jnp.broadcast_to(jax.nn.log_sigmoid(dl_ref[1, h])[0:1, :], (C, C))
        decs.append(dict(
            mask=jnp.exp(jnp.where(rel >= 0, lf * rel, -lb * rel)),
            qdf=jnp.exp(lf * (row + 1.0)), kdf=jnp.exp(lf * (C - 1.0 - row)),
            qdb=jnp.exp(lb * (C - row)), kdb=jnp.exp(lb * row),
            cf=jnp.exp(lf * C), cb=jnp.exp(lb * C)))

    st_s[...] = jnp.zeros_like(st_s)

    def step(i, carry):
        rf = pl.ds(pl.multiple_of(i * C, C), C)
        rb = pl.ds(pl.multiple_of((n_chunks - 1 - i) * C, C), C)
        for h in range(HEADS):
            hs = slice(h * HEAD_D, (h + 1) * HEAD_D)
            d = decs[h]
            qi = qr_s[rf, hs]
            ki = kr_s[rf, hs]
            vi = v_ref[rf, hs]
            sf = st_s[h]
            sc = _dot_nt(qi.astype(BF16), ki.astype(BF16)) * d["mask"]
            of_s[rf, hs] = _dot(sc.astype(BF16), vi) + _dot((qi * d["qdf"]).astype(BF16), sf.astype(BF16))
            st_s[h] = sf * d["cf"] + _dot_tn((ki * d["kdf"]).astype(BF16), vi)
            qj = qr_s[rb, hs]
            kj = kr_s[rb, hs]
            vj = v_ref[rb, hs]
            sb = st_s[HEADS + h]
            ob_s[rb, hs] = _dot((qj * d["qdb"]).astype(BF16), sb.astype(BF16))
            st_s[HEADS + h] = sb * d["cb"] + _dot_tn((kj * d["kdb"]).astype(BF16), vj)
        return carry

    lax.fori_loop(0, n_chunks, step, 0)

    def finish(i, carry):
        r = pl.ds(pl.multiple_of(i * C, C), C)
        for h in range(HEADS):
            hs = slice(h * HEAD_D, (h + 1) * HEAD_D)
            o = of_s[r, hs] + ob_s[r, hs]
            mu = jnp.mean(o, axis=-1, keepdims=True)
            oc = o - mu
            var = jnp.mean(oc * oc, axis=-1, keepdims=True)
            on = oc * lax.rsqrt(var + EPS) * ng_ref[:, hs]
            y_ref[r, hs] = (on * _silu(z_ref[r, hs].astype(F32))).astype(y_ref.dtype)
        return carry

    lax.fori_loop(0, n_chunks, finish, 0)


def _retention(proj, cos, sin, dl, ng):
    col = lambda c: pl.BlockSpec((SEQ, BRANCH_W), lambda b, c=c: (b, c))
    full = lambda shape: pl.BlockSpec(shape, lambda b: (0,) * len(shape))
    return pl.pallas_call(
        _ret_body,
        grid=(BATCH,),
        in_specs=[col(COL_RQ), col(COL_RK), col(COL_RV), col(COL_RZ),
                  full((SEQ, HEAD_D)), full((SEQ, HEAD_D)), full((2, HEADS, 8, LANES)), full((1, BRANCH_W))],
        out_specs=pl.BlockSpec((SEQ, BRANCH_W), lambda b: (b, 0)),
        out_shape=jax.ShapeDtypeStruct((TOKENS, BRANCH_W), BF16),
        scratch_shapes=[pltpu.VMEM((SEQ, BRANCH_W), F32), pltpu.VMEM((SEQ, BRANCH_W), F32),
                        pltpu.VMEM((SEQ, BRANCH_W), F32), pltpu.VMEM((SEQ, BRANCH_W), F32),
                        pltpu.VMEM((2 * HEADS, HEAD_D, HEAD_D), F32)],
        compiler_params=_params("arbitrary"),
        name="retention",
    )(proj, proj, proj, proj, cos, sin, dl, ng)


def _gdn_body(q_ref, k_ref, v_ref, z_ref, ab_ref, cw_ref, alog_ref, dtb_ref, ng_ref, y_ref,
              qn_s, kn_s, vn_s, g_s, gcf_s, gcb_s, of_s, ob_s, st_s):
    C = GDN_CHUNK
    n_chunks = SEQ // C
    srow = lax.broadcasted_iota(jnp.int32, (SEQ, LANES), 0)

    def conv_silu(x_ref, hs, wofs):
        x = x_ref[:, hs].astype(F32)
        w = cw_ref[:, wofs + hs.start: wofs + hs.stop]
        prev = jnp.where(srow == 0, 0.0, pltpu.roll(x, 1, 0))
        nxt = jnp.where(srow == SEQ - 1, 0.0, pltpu.roll(x, SEQ - 1, 0))
        return _silu(prev * w[0:1, :] + x * w[1:2, :] + nxt * w[2:3, :])

    for h in range(HEADS):
        hs = slice(h * HEAD_D, (h + 1) * HEAD_D)
        q = conv_silu(q_ref, hs, 0)
        qn_s[:, hs] = (q * lax.rsqrt(jnp.sum(q * q, axis=-1, keepdims=True) + EPS) * (HEAD_D ** -0.5)).astype(qn_s.dtype)
        k = conv_silu(k_ref, hs, BRANCH_W)
        kn_s[:, hs] = (k * lax.rsqrt(jnp.sum(k * k, axis=-1, keepdims=True) + EPS)).astype(kn_s.dtype)
        vn_s[:, hs] = conv_silu(v_ref, hs, 2 * BRANCH_W).astype(vn_s.dtype)

    ab = ab_ref[...]
    lane = lax.broadcasted_iota(jnp.int32, (SEQ, LANES), 1)
    g_all = jnp.where((lane & 7) < HEADS,
                      -jnp.exp(alog_ref[...]) * jax.nn.softplus(ab + dtb_ref[...]),
                      jax.nn.sigmoid(ab))
    g_s[...] = g_all
    pos = srow & (C - 1)
    gf = g_all
    gb = g_all
    s = 1
    while s < C:
        gf = gf + jnp.where(pos >= s, pltpu.roll(gf, s, 0), 0.0)
        gb = gb + jnp.where(pos < C - s, pltpu.roll(gb, SEQ - s, 0), 0.0)
        s *= 2
    gcf_s[...] = gf
    gcb_s[...] = gb

    ri = lax.broadcasted_iota(jnp.int32, (C, C), 0)
    ci = lax.broadcasted_iota(jnp.int32, (C, C), 1)
    eye = (ri == ci).astype(F32)
    masks = ((ri >= ci, ri > ci), (ri <= ci, ri < ci))
    hi = lax.Precision.HIGHEST

    st_s[...] = jnp.zeros_like(st_s)

    def chunk_step(c, d):
        r = pl.ds(pl.multiple_of(c * C, C), C)
        gc = (gcf_s if d == 0 else gcb_s)[r, :]
        gbeta = g_s[r, :]
        gct = gc.T
        incl, strict = masks[d]
        o_s = of_s if d == 0 else ob_s
        last = C - 1 if d == 0 else 0
        for h in range(HEADS):
            hs = slice(h * HEAD_D, (h + 1) * HEAD_D)
            gl, bl = 8 * d + h, 8 * d + HEADS + h
            gcol = jnp.broadcast_to(gc[:, gl:gl + 1], (C, HEAD_D))
            beta = jnp.broadcast_to(gbeta[:, bl:bl + 1], (C, HEAD_D))
            grow = jnp.broadcast_to(gct[gl:gl + 1, :], (C, C))
            glast = gcol[last:last + 1, :]
            q = qn_s[r, hs].astype(F32)
            k = kn_s[r, hs].astype(F32)
            v = vn_s[r, hs].astype(F32)
            kb = k * beta
            decay = jnp.exp(jnp.where(incl, gcol[:, :C] - grow, -jnp.inf))
            kk = _dot_nt(jnp.concatenate([kb, q], axis=0).astype(BF16), k.astype(BF16))
            a = jnp.where(strict, -kk[:C] * decay, 0.0)
            scores = kk[C:] * decay
            p = eye + a
            xp = a
            for _ in range(5):
                xp = _dot(xp, xp, hi)
                p = p + _dot(p, xp, hi)
            eg = jnp.exp(gcol)
            rhs = jnp.concatenate([v * beta, kb * eg], axis=1).astype(BF16)
            sol = _dot(p.astype(BF16), rhs)
            u = sol[:, :HEAD_D]
            w = sol[:, HEAD_D:]
            st = st_s[d * HEADS + h]
            ws = _dot(jnp.concatenate([w, q * eg], axis=0).astype(BF16), st.astype(BF16))
            v_new = u - ws[:C]
            o_s[r, hs] = ws[C:] + _dot(scores.astype(BF16), v_new.astype(BF16))
            kd = k * jnp.exp(glast - gcol)
            st_s[d * HEADS + h] = st * jnp.exp(glast) + _dot_tn(kd.astype(BF16), v_new.astype(BF16))

    def step(i, carry):
        chunk_step(i, 0)
        chunk_step(n_chunks - 1 - i, 1)
        return carry

    lax.fori_loop(0, n_chunks, step, 0)

    def finish(i, carry):
        r = pl.ds(pl.multiple_of(i * 2 * C, 2 * C), 2 * C)
        for h in range(HEADS):
            hs = slice(h * HEAD_D, (h + 1) * HEAD_D)
            o = of_s[r, hs] + ob_s[r, hs]
            on = o * lax.rsqrt(jnp.mean(o * o, axis=-1, keepdims=True) + EPS) * ng_ref[...]
            y_ref[r, hs] = (on * _silu(z_ref[r, hs].astype(F32))).astype(y_ref.dtype)
        return carry

    lax.fori_loop(0, n_chunks // 2, finish, 0)


def _gdn(proj, ab, conv_w, alog_row, dtb_row, ng):
    col = lambda c: pl.BlockSpec((SEQ, BRANCH_W), lambda b, c=c: (b, c))
    full = lambda shape: pl.BlockSpec(shape, lambda b: (0,) * len(shape))
    big = lambda dt: pltpu.VMEM((SEQ, BRANCH_W), dt)
    small = pltpu.VMEM((SEQ, LANES), F32)
    return pl.pallas_call(
        _gdn_body,
        grid=(BATCH,),
        in_specs=[col(COL_DQ), col(COL_DK), col(COL_DV), col(COL_DZ),
                  pl.BlockSpec((SEQ, LANES), lambda b: (b, 0)),
                  full((3, 3 * BRANCH_W)), full((1, LANES)), full((1, LANES)), full((1, HEAD_D))],
        out_specs=pl.BlockSpec((SEQ, BRANCH_W), lambda b: (b, 0)),
        out_shape=jax.ShapeDtypeStruct((TOKENS, BRANCH_W), BF16),
        scratch_shapes=[big(BF16), big(BF16), big(BF16), small, small, small, big(F32), big(F32),
                        pltpu.VMEM((2 * HEADS, HEAD_D, HEAD_D), F32)],
        compiler_params=_params("arbitrary"),
        name="gdn",
    )(proj, proj, proj, proj, ab, conv_w, alog_row, dtb_row, ng)


def _merge_body(final, x_ref, g_ref, ya_ref, yb_ref, yc_ref, wg_ref, wbo_ref, wo_ref, *rest):
    if final:
        fg_ref, o_ref = rest
    else:
        (o_ref,) = rest
    x = x_ref[...]
    h = (x * lax.rsqrt(jnp.mean(x * x, axis=-1, keepdims=True) + EPS) * g_ref[...]).astype(BF16)
    merged = None
    for n, y_ref in enumerate((ya_ref, yb_ref, yc_ref)):
        gate = jax.nn.sigmoid(_dot(h, wg_ref[n]))
        term = gate * _dot(y_ref[...], wbo_ref[n])
        merged = term if merged is None else merged + term
    out = x + _dot(merged.astype(BF16), wo_ref[...])
    if final:
        out = out * lax.rsqrt(jnp.mean(out * out, axis=-1, keepdims=True) + EPS) * fg_ref[...]
    o_ref[...] = out


def _merge(x, g, ya, yb, yc, wg, wbo, wo, final_g=None, tm=512):
    final = final_g is not None
    row = lambda w: pl.BlockSpec((tm, w), lambda i: (i, 0))
    full = lambda shape: pl.BlockSpec(shape, lambda i: (0,) * len(shape))
    in_specs = [row(D_MODEL), full((1, D_MODEL)), row(BRANCH_W), row(BRANCH_W), row(BRANCH_W),
                full((N_BRANCH, D_MODEL, D_MODEL)), full((N_BRANCH, BRANCH_W, D_MODEL)), full((D_MODEL, D_MODEL))]
    args = [x, g, ya, yb, yc, wg, wbo, wo]
    if final:
        in_specs.append(full((1, D_MODEL)))
        args.append(final_g)
    return pl.pallas_call(
        functools.partial(_merge_body, final),
        grid=(TOKENS // tm,),
        in_specs=in_specs,
        out_specs=row(D_MODEL),
        out_shape=jax.ShapeDtypeStruct((TOKENS, D_MODEL), F32),
        compiler_params=_params("arbitrary"),
        name="merge_final" if final else "merge",
    )(*args)


def _rope_tables():
    inv = ROPE_BASE ** (-jnp.arange(0, HEAD_D, 2, dtype=F32) / HEAD_D)
    ang = jnp.arange(SEQ, dtype=F32)[:, None] * inv[None, :]
    cos = jnp.cos(ang)
    sin = jnp.sin(ang)
    return jnp.concatenate([cos, cos], axis=-1), jnp.concatenate([-sin, sin], axis=-1)


def _lane_row(p):
    row = jnp.zeros((LANES,), F32)
    row = row.at[0:HEADS].set(p[0]).at[2 * HEADS:3 * HEADS].set(p[1])
    return row[None, :]


def kernel(x, norm_g, w_in, gm_ln_g, gm_ln_b, gm_w_s, gm_b_s, ret_decay_logit, ret_norm_g, gdn_conv_w, gdn_a_log, gdn_dt_bias, gdn_norm_g, w_gate, w_branch_out, w_out, final_norm_g):
    xf = x.reshape(TOKENS, D_MODEL)
    cos, sin = _rope_tables()
    for l in range(DEPTH):
        w_main = w_in[l][:, :MAIN_COLS].astype(BF16)
        w_ab = jnp.pad(w_in[l][:, MAIN_COLS:], ((0, 0), (0, LANES - 4 * HEADS))).astype(BF16)
        g_row = norm_g[l][None, :]
        proj, ab = _in_proj(xf, g_row, w_main, w_ab)

        bias = jnp.repeat(gm_b_s[l].T, GM_CHUNK, axis=1)
        ya = _gmlp(proj, gm_ln_g[l][None, :], gm_ln_b[l][None, :], gm_w_s[l].astype(BF16), bias)

        dl = jnp.broadcast_to(ret_decay_logit[l][:, :, None, None], (2, HEADS, 8, LANES))
        yb = _retention(proj, cos, sin, dl, ret_norm_g[l][None, :])

        yc = _gdn(proj, ab, gdn_conv_w[l], _lane_row(gdn_a_log[l]), _lane_row(gdn_dt_bias[l]),
                  gdn_norm_g[l][None, :])

        wg = jnp.transpose(w_gate[l], (1, 0, 2)).astype(BF16)
        xf = _merge(xf, g_row, ya, yb, yc, wg, w_branch_out[l].astype(BF16), w_out[l].astype(BF16),
                    final_norm_g[None, :] if l == DEPTH - 1 else None)
    return xf.reshape(BATCH, SEQ, D_MODEL)
```

```python
import functools

import jax
import jax.numpy as jnp
from jax import lax
from jax.experimental import pallas as pl
from jax.experimental.pallas import tpu as pltpu

F32 = jnp.float32
BF16 = jnp.bfloat16

D_MODEL = 1024
BATCH = 16
SEQ = 2048
DEPTH = 2
TOKENS = BATCH * SEQ
BRANCH_W = 512
N_BRANCH = 3
GM_GROUPS = 4
GM_CHUNK = 128
HEADS = 4
HEAD_D = 128
RET_CHUNK = 128
GDN_CHUNK = 64
ROPE_BASE = 10000.0
EPS = 1e-6
LANES = 128
MAIN_COLS = 11 * BRANCH_W
COL_GU, COL_GV, COL_GZ = 0, 1, 2
COL_RQ, COL_RK, COL_RV, COL_RZ = 3, 4, 5, 6
COL_DQ, COL_DK, COL_DV, COL_DZ = 7, 8, 9, 10

VMEM_LIMIT = 56 * 1024 * 1024


def _dot(a, b, precision=None):
    return jnp.dot(a, b, preferred_element_type=F32, precision=precision)


def _dot_nt(a, b):
    return lax.dot_general(a, b, (((1,), (1,)), ((), ())), preferred_element_type=F32)


def _dot_tn(a, b):
    return lax.dot_general(a, b, (((0,), (0,)), ((), ())), preferred_element_type=F32)


def _silu(x):
    return x * jax.nn.sigmoid(x)


def _gelu(x):
    return 0.5 * x * (1.0 + lax.erf(x * 0.7071067811865476))


def _params(*sem):
    return pltpu.CompilerParams(dimension_semantics=sem, vmem_limit_bytes=VMEM_LIMIT)


def _inproj_body(x_ref, g_ref, w_ref, wab_ref, o_ref, ab_ref, h_ref):
    @pl.when(pl.program_id(1) == 0)
    def _():
        x = x_ref[...]
        h = x * lax.rsqrt(jnp.mean(x * x, axis=-1, keepdims=True) + EPS) * g_ref[...]
        hb = h.astype(BF16)
        h_ref[...] = hb
        ab_ref[...] = _dot(hb, wab_ref[...])

    o_ref[...] = _dot(h_ref[...], w_ref[...]).astype(o_ref.dtype)


def _in_proj(x, g, w_main, w_ab, tm=1024, tn=1408):
    grid = (TOKENS // tm, MAIN_COLS // tn)
    return pl.pallas_call(
        _inproj_body,
        grid=grid,
        in_specs=[
            pl.BlockSpec((tm, D_MODEL), lambda i, j: (i, 0)),
            pl.BlockSpec((1, D_MODEL), lambda i, j: (0, 0)),
            pl.BlockSpec((D_MODEL, tn), lambda i, j: (0, j)),
            pl.BlockSpec((D_MODEL, LANES), lambda i, j: (0, 0)),
        ],
        out_specs=[
            pl.BlockSpec((tm, tn), lambda i, j: (i, j)),
            pl.BlockSpec((tm, LANES), lambda i, j: (i, 0)),
        ],
        out_shape=[
            jax.ShapeDtypeStruct((TOKENS, MAIN_COLS), BF16),
            jax.ShapeDtypeStruct((TOKENS, LANES), F32),
        ],
        scratch_shapes=[pltpu.VMEM((tm, D_MODEL), BF16)],
        compiler_params=_params("arbitrary", "arbitrary"),
        name="in_proj",
    )(x, g, w_main, w_ab)


def _gmlp_body(u_ref, v_ref, z_ref, lng_ref, lnb_ref, ws_ref, bias_ref, o_ref):
    rows = u_ref.shape[0]
    for c in range(rows // GM_CHUNK):
        r = slice(c * GM_CHUNK, (c + 1) * GM_CHUNK)
        u = _gelu(u_ref[r, :].astype(F32))
        v = _gelu(v_ref[r, :].astype(F32))
        z = z_ref[r, :].astype(F32)
        mu = jnp.mean(v, axis=-1, keepdims=True)
        vc = v - mu
        var = jnp.mean(vc * vc, axis=-1, keepdims=True)
        vn = (vc * lax.rsqrt(var + EPS) * lng_ref[...] + lnb_ref[...]).astype(BF16)
        gate = u * _silu(z)
        for g in range(GM_GROUPS):
            cs = slice(g * LANES, (g + 1) * LANES)
            mixed = _dot(ws_ref[g], vn[:, cs]) + bias_ref[:, cs]
            o_ref[r, cs] = (gate[:, cs] * mixed).astype(o_ref.dtype)


def _gmlp(proj, ln_g, ln_b, w_s, bias, tm=512):
    col = lambda c: pl.BlockSpec((tm, BRANCH_W), lambda i, c=c: (i, c))
    full = lambda shape: pl.BlockSpec(shape, lambda i: (0,) * len(shape))
    return pl.pallas_call(
        _gmlp_body,
        grid=(TOKENS // tm,),
        in_specs=[col(COL_GU), col(COL_GV), col(COL_GZ), full((1, BRANCH_W)), full((1, BRANCH_W)),
                  full((GM_GROUPS, GM_CHUNK, GM_CHUNK)), full((GM_CHUNK, BRANCH_W))],
        out_specs=pl.BlockSpec((tm, BRANCH_W), lambda i: (i, 0)),
        out_shape=jax.ShapeDtypeStruct((TOKENS, BRANCH_W), BF16),
        compiler_params=_params("arbitrary"),
        name="gmlp",
    )(proj, proj, proj, ln_g, ln_b, w_s, bias)


def _ret_body(q_ref, k_ref, v_ref, z_ref, cos_ref, sin_ref, dl_ref, ng_ref, y_ref,
              qr_s, kr_s, of_s, ob_s, st_s):
    C = RET_CHUNK
    n_chunks = SEQ // C
    row = lax.broadcasted_iota(jnp.int32, (C, C), 0).astype(F32)
    colm = lax.broadcasted_iota(jnp.int32, (C, C), 1).astype(F32)
    rel = row - colm

    cos = cos_ref[...]
    sin = sin_ref[...]
    for h in range(HEADS):
        hs = slice(h * HEAD_D, (h + 1) * HEAD_D)
        q = q_ref[:, hs].astype(F32)
        k = k_ref[:, hs].astype(F32)
        qr_s[:, hs] = q * cos + pltpu.roll(q, HEAD_D // 2, 1) * sin
        kr_s[:, hs] = (k * cos + pltpu.roll(k, HEAD_D // 2, 1) * sin) * (HEAD_D ** -0.5)

    decs = []
    for h in range(HEADS):
        lf = jnp.broadcast_to(jax.nn.log_sigmoid(dl_ref[0, h])[0:1, :], (C, C))
        lb = jnp.broadcast_to(jax.nn.log_sigmoid(dl_ref[1, h])[0:1, :], (C, C))
        decs.append(dict(
            mask=jnp.exp(jnp.where(rel >= 0, lf * rel, -lb * rel)),
            qdf=jnp.exp(lf * (row + 1.0)), kdf=jnp.exp(lf * (C - 1.0 - row)),
            qdb=jnp.exp(lb * (C - row)), kdb=jnp.exp(lb * row),
            cf=jnp.exp(lf * C), cb=jnp.exp(lb * C)))

    st_s[...] = jnp.zeros_like(st_s)

    def step(i, carry):
        rf = pl.ds(pl.multiple_of(i * C, C), C)
        rb = pl.ds(pl.multiple_of((n_chunks - 1 - i) * C, C), C)
        for h in range(HEADS):
            hs = slice(h * HEAD_D, (h + 1) * HEAD_D)
            d = decs[h]
            qi = qr_s[rf, hs]
            ki = kr_s[rf, hs]
            vi = v_ref[rf, hs]
            sf = st_s[h]
            sc = _dot_nt(qi.astype(BF16), ki.astype(BF16)) * d["mask"]
            of_s[rf, hs] = _dot(sc.astype(BF16), vi) + _dot((qi * d["qdf"]).astype(BF16), sf.astype(BF16))
            st_s[h] = sf * d["cf"] + _dot_tn((ki * d["kdf"]).astype(BF16), vi)
            qj = qr_s[rb, hs]
            kj = kr_s[rb, hs]
            vj = v_ref[rb, hs]
            sb = st_s[HEADS + h]
            ob_s[rb, hs] = _dot((qj * d["qdb"]).astype(BF16), sb.astype(BF16))
            st_s[HEADS + h] = sb * d["cb"] + _dot_tn((kj * d["kdb"]).astype(BF16), vj)
        return carry

    lax.fori_loop(0, n_chunks, step, 0)

    def finish(i, carry):
        r = pl.ds(pl.multiple_of(i * C, C), C)
        for h in range(HEADS):
            hs = slice(h * HEAD_D, (h + 1) * HEAD_D)
            o = of_s[r, hs] + ob_s[r, hs]
            mu = jnp.mean(o, axis=-1, keepdims=True)
            oc = o - mu
            var = jnp.mean(oc * oc, axis=-1, keepdims=True)
            on = oc * lax.rsqrt(var + EPS) * ng_ref[:, hs]
            y_ref[r, hs] = (on * _silu(z_ref[r, hs].astype(F32))).astype(y_ref.dtype)
        return carry

    lax.fori_loop(0, n_chunks, finish, 0)


def _retention(proj, cos, sin, dl, ng):
    col = lambda c: pl.BlockSpec((SEQ, BRANCH_W), lambda b, c=c: (b, c))
    full = lambda shape: pl.BlockSpec(shape, lambda b: (0,) * len(shape))
    return pl.pallas_call(
        _ret_body,
        grid=(BATCH,),
        in_specs=[col(COL_RQ), col(COL_RK), col(COL_RV), col(COL_RZ),
                  full((SEQ, HEAD_D)), full((SEQ, HEAD_D)), full((2, HEADS, 8, LANES)), full((1, BRANCH_W))],
        out_specs=pl.BlockSpec((SEQ, BRANCH_W), lambda b: (b, 0)),
        out_shape=jax.ShapeDtypeStruct((TOKENS, BRANCH_W), BF16),
        scratch_shapes=[pltpu.VMEM((SEQ, BRANCH_W), F32), pltpu.VMEM((SEQ, BRANCH_W), F32),
                        pltpu.VMEM((SEQ, BRANCH_W), F32), pltpu.VMEM((SEQ, BRANCH_W), F32),
                        pltpu.VMEM((2 * HEADS, HEAD_D, HEAD_D), F32)],
        compiler_params=_params("arbitrary"),
        name="retention",
    )(proj, proj, proj, proj, cos, sin, dl, ng)


def _gdn_body(q_ref, k_ref, v_ref, z_ref, ab_ref, cw_ref, alog_ref, dtb_ref, ng_ref, y_ref,
              qn_s, kn_s, vn_s, g_s, gcf_s, gcb_s, u_s, w_s, sc_s, of_s, ob_s, st_s):
    C = GDN_CHUNK
    n_chunks = SEQ // C
    srow = lax.broadcasted_iota(jnp.int32, (SEQ, LANES), 0)

    def conv_silu(x_ref, hs, wofs):
        x = x_ref[:, hs].astype(F32)
        w = cw_ref[:, wofs + hs.start: wofs + hs.stop]
        prev = jnp.where(srow == 0, 0.0, pltpu.roll(x, 1, 0))
        nxt = jnp.where(srow == SEQ - 1, 0.0, pltpu.roll(x, SEQ - 1, 0))
        return _silu(prev * w[0:1, :] + x * w[1:2, :] + nxt * w[2:3, :])

    for h in range(HEADS):
        hs = slice(h * HEAD_D, (h + 1) * HEAD_D)
        q = conv_silu(q_ref, hs, 0)
        qn_s[:, hs] = (q * lax.rsqrt(jnp.sum(q * q, axis=-1, keepdims=True) + EPS) * (HEAD_D ** -0.5)).astype(qn_s.dtype)
        k = conv_silu(k_ref, hs, BRANCH_W)
        kn_s[:, hs] = (k * lax.rsqrt(jnp.sum(k * k, axis=-1, keepdims=True) + EPS)).astype(kn_s.dtype)
        vn_s[:, hs] = conv_silu(v_ref, hs, 2 * BRANCH_W).astype(vn_s.dtype)

    ab = ab_ref[...]
    lane = lax.broadcasted_iota(jnp.int32, (SEQ, LANES), 1)
    g_all = jnp.where((lane & 7) < HEADS,
                      -jnp.exp(alog_ref[...]) * jax.nn.softplus(ab + dtb_ref[...]),
                      jax.nn.sigmoid(ab))
    g_s[...] = g_all
    pos = srow & (C - 1)
    gf = g_all
    gb = g_all
    s = 1
    while s < C:
        gf = gf + jnp.where(pos >= s, pltpu.roll(gf, s, 0), 0.0)
        gb = gb + jnp.where(pos < C - s, pltpu.roll(gb, SEQ - s, 0), 0.0)
        s *= 2
    gcf_s[...] = gf
    gcb_s[...] = gb

    ri = lax.broadcasted_iota(jnp.int32, (C, C), 0)
    ci = lax.broadcasted_iota(jnp.int32, (C, C), 1)
    eye = (ri == ci).astype(F32)
    masks = ((ri >= ci, ri > ci), (ri <= ci, ri < ci))
    levels = [((ri >> (j + 1)) == (ci >> (j + 1))) & ((ri >> j) != (ci >> j)) for j in range(6)]
    chains = [(d, h) for d in range(2) for h in range(HEADS)]
    gc_refs = (gcf_s, gcb_s)

    def head_slice(h):
        return slice(h * HEAD_D, (h + 1) * HEAD_D)

    def phase_a(c, carry):
        r = pl.ds(pl.multiple_of(c * C, C), C)
        gcs = [gc_refs[d][r, :] for d in range(2)]
        gts = [g.T for g in gcs]
        gbeta = g_s[r, :]
        kb16 = [kn_s[r, head_slice(h)] for h in range(HEADS)]
        kks = [_dot_nt(jnp.concatenate([kb16[h], qn_s[r, head_slice(h)]], axis=0), kb16[h]) for h in range(HEADS)]
        tri, rhs = [], []
        for d, h in chains:
            incl, strict = masks[d]
            gl, bl = 8 * d + h, 8 * d + HEADS + h
            gcol = jnp.broadcast_to(gcs[d][:, gl:gl + 1], (C, HEAD_D))
            beta = jnp.broadcast_to(gbeta[:, bl:bl + 1], (C, HEAD_D))
            grow = jnp.broadcast_to(gts[d][gl:gl + 1, :], (C, C))
            decay = jnp.exp(jnp.where(incl, gcol[:, :C] - grow, -jnp.inf))
            tri.append(jnp.where(strict, beta[:, :C] * kks[h][:C] * decay, 0.0))
            sc_s[d, h, r, :] = (kks[h][C:] * decay).astype(sc_s.dtype)
            kbeta = kb16[h].astype(F32) * beta
            rhs.append(jnp.concatenate([vn_s[r, head_slice(h)].astype(F32) * beta, kbeta * jnp.exp(gcol)],
                                       axis=1).astype(BF16))
        inv = [eye - jnp.where(levels[0], t, 0.0) for t in tri]
        for j in range(1, 6):
            m16 = [jnp.where(levels[j], t, 0.0).astype(BF16) for t in tri]
            i16 = [p.astype(BF16) for p in inv]
            mi = [_dot(m, p).astype(BF16) for m, p in zip(m16, i16)]
            inv = [p - _dot(p16, x) for p, p16, x in zip(inv, i16, mi)]
        for (d, h), p, b in zip(chains, inv, rhs):
            sol = _dot(p.astype(BF16), b)
            u_s[d, r, head_slice(h)] = sol[:, :HEAD_D].astype(u_s.dtype)
            w_s[d, r, head_slice(h)] = sol[:, HEAD_D:].astype(w_s.dtype)
        return carry

    lax.fori_loop(0, n_chunks, phase_a, 0)

    st_s[...] = jnp.zeros_like(st_s)
    o_refs = (of_s, ob_s)

    def phase_b(i, carry):
        cs = (i, n_chunks - 1 - i)
        rs = [pl.ds(pl.multiple_of(c * C, C), C) for c in cs]
        gcs = [gc_refs[d][rs[d], :] for d in range(2)]
        lhs, kd, egl = [], [], []
        for d, h in chains:
            r, hs = rs[d], head_slice(h)
            gl = 8 * d + h
            gcol = jnp.broadcast_to(gcs[d][:, gl:gl + 1], (C, HEAD_D))
            last = C - 1 if d == 0 else 0
            glast = gcol[last:last + 1, :]
            qe = (qn_s[r, hs].astype(F32) * jnp.exp(gcol)).astype(BF16)
            lhs.append(jnp.concatenate([w_s[d, r, hs], qe], axis=0))
            kd.append((kn_s[r, hs].astype(F32) * jnp.exp(glast - gcol)).astype(BF16))
            egl.append(jnp.exp(glast))
        st = [st_s[n] for n in range(len(chains))]
        ws = [_dot(a, s_.astype(BF16)) for a, s_ in zip(lhs, st)]
        vnew = [(u_s[d, rs[d], head_slice(h)].astype(F32) - x[:C]).astype(BF16) for (d, h), x in zip(chains, ws)]
        for n, (d, h) in enumerate(chains):
            o_refs[d][rs[d], head_slice(h)] = ws[n][C:] + _dot(sc_s[d, h, rs[d], :], vnew[n])
            st_s[n] = st[n] * egl[n] + _dot_tn(kd[n], vnew[n])
        return carry

    lax.fori_loop(0, n_chunks, phase_b, 0)

    def finish(i, carry):
        r = pl.ds(pl.multiple_of(i * 2 * C, 2 * C), 2 * C)
        for h in range(HEADS):
            hs = head_slice(h)
            o = of_s[r, hs] + ob_s[r, hs]
            on = o * lax.rsqrt(jnp.mean(o * o, axis=-1, keepdims=True) + EPS) * ng_ref[...]
            y_ref[r, hs] = (on * _silu(z_ref[r, hs].astype(F32))).astype(y_ref.dtype)
        return carry

    lax.fori_loop(0, n_chunks // 2, finish, 0)


def _gdn(proj, ab, conv_w, alog_row, dtb_row, ng):
    col = lambda c: pl.BlockSpec((SEQ, BRANCH_W), lambda b, c=c: (b, c), pipeline_mode=pl.Buffered(1))
    full = lambda shape: pl.BlockSpec(shape, lambda b: (0,) * len(shape))
    big = lambda dt: pltpu.VMEM((SEQ, BRANCH_W), dt)
    small = pltpu.VMEM((SEQ, LANES), F32)
    return pl.pallas_call(
        _gdn_body,
        grid=(BATCH,),
        in_specs=[col(COL_DQ), col(COL_DK), col(COL_DV), col(COL_DZ),
                  pl.BlockSpec((SEQ, LANES), lambda b: (b, 0)),
                  full((3, 3 * BRANCH_W)), full((1, LANES)), full((1, LANES)), full((1, HEAD_D))],
        out_specs=pl.BlockSpec((SEQ, BRANCH_W), lambda b: (b, 0)),
        out_shape=jax.ShapeDtypeStruct((TOKENS, BRANCH_W), BF16),
        scratch_shapes=[big(BF16), big(BF16), big(BF16), small, small, small,
                        pltpu.VMEM((2, SEQ, BRANCH_W), BF16), pltpu.VMEM((2, SEQ, BRANCH_W), BF16),
                        pltpu.VMEM((2, HEADS, SEQ, GDN_CHUNK), BF16),
                        big(F32), big(F32), pltpu.VMEM((2 * HEADS, HEAD_D, HEAD_D), F32)],
        compiler_params=_params("arbitrary"),
        name="gdn",
    )(proj, proj, proj, proj, ab, conv_w, alog_row, dtb_row, ng)


def _merge_body(final, x_ref, g_ref, ya_ref, yb_ref, yc_ref, wg_ref, wbo_ref, wo_ref, *rest):
    if final:
        fg_ref, o_ref = rest
    else:
        (o_ref,) = rest
    x = x_ref[...]
    h = (x * lax.rsqrt(jnp.mean(x * x, axis=-1, keepdims=True) + EPS) * g_ref[...]).astype(BF16)
    merged = None
    for n, y_ref in enumerate((ya_ref, yb_ref, yc_ref)):
        gate = jax.nn.sigmoid(_dot(h, wg_ref[n]))
        term = gate * _dot(y_ref[...], wbo_ref[n])
        merged = term if merged is None else merged + term
    out = x + _dot(merged.astype(BF16), wo_ref[...])
    if final:
        out = out * lax.rsqrt(jnp.mean(out * out, axis=-1, keepdims=True) + EPS) * fg_ref[...]
    o_ref[...] = out


def _merge(x, g, ya, yb, yc, wg, wbo, wo, final_g=None, tm=512):
    final = final_g is not None
    row = lambda w: pl.BlockSpec((tm, w), lambda i: (i, 0))
    full = lambda shape: pl.BlockSpec(shape, lambda i: (0,) * len(shape))
    in_specs = [row(D_MODEL), full((1, D_MODEL)), row(BRANCH_W), row(BRANCH_W), row(BRANCH_W),
                full((N_BRANCH, D_MODEL, D_MODEL)), full((N_BRANCH, BRANCH_W, D_MODEL)), full((D_MODEL, D_MODEL))]
    args = [x, g, ya, yb, yc, wg, wbo, wo]
    if final:
        in_specs.append(full((1, D_MODEL)))
        args.append(final_g)
    return pl.pallas_call(
        functools.partial(_merge_body, final),
        grid=(TOKENS // tm,),
        in_specs=in_specs,
        out_specs=row(D_MODEL),
        out_shape=jax.ShapeDtypeStruct((TOKENS, D_MODEL), F32),
        compiler_params=_params("arbitrary"),
        name="merge_final" if final else "merge",
    )(*args)


def _rope_tables():
    inv = ROPE_BASE ** (-jnp.arange(0, HEAD_D, 2, dtype=F32) / HEAD_D)
    ang = jnp.arange(SEQ, dtype=F32)[:, None] * inv[None, :]
    cos = jnp.cos(ang)
    sin = jnp.sin(ang)
    return jnp.concatenate([cos, cos], axis=-1), jnp.concatenate([-sin, sin], axis=-1)


def _lane_row(p):
    row = jnp.zeros((LANES,), F32)
    row = row.at[0:HEADS].set(p[0]).at[2 * HEADS:3 * HEADS].set(p[1])
    return row[None, :]


def kernel(x, norm_g, w_in, gm_ln_g, gm_ln_b, gm_w_s, gm_b_s, ret_decay_logit, ret_norm_g, gdn_conv_w, gdn_a_log, gdn_dt_bias, gdn_norm_g, w_gate, w_branch_out, w_out, final_norm_g):
    xf = x.reshape(TOKENS, D_MODEL)
    cos, sin = _rope_tables()
    for l in range(DEPTH):
        w_main = w_in[l][:, :MAIN_COLS].astype(BF16)
        w_ab = jnp.pad(w_in[l][:, MAIN_COLS:], ((0, 0), (0, LANES - 4 * HEADS))).astype(BF16)
        g_row = norm_g[l][None, :]
        proj, ab = _in_proj(xf, g_row, w_main, w_ab)

        bias = jnp.repeat(gm_b_s[l].T, GM_CHUNK, axis=1)
        ya = _gmlp(proj, gm_ln_g[l][None, :], gm_ln_b[l][None, :], gm_w_s[l].astype(BF16), bias)

        dl = jnp.broadcast_to(ret_decay_logit[l][:, :, None, None], (2, HEADS, 8, LANES))
        yb = _retention(proj, cos, sin, dl, ret_norm_g[l][None, :])

        yc = _gdn(proj, ab, gdn_conv_w[l], _lane_row(gdn_a_log[l]), _lane_row(gdn_dt_bias[l]),
                  gdn_norm_g[l][None, :])

        wg = jnp.transpose(w_gate[l], (1, 0, 2)).astype(BF16)
        xf = _merge(xf, g_row, ya, yb, yc, wg, w_branch_out[l].astype(BF16), w_out[l].astype(BF16),
                    final_norm_g[None, :] if l == DEPTH - 1 else None)
    return xf.reshape(BATCH, SEQ, D_MODEL)
```

```python
import functools

import jax
import jax.numpy as jnp
from jax import lax
from jax.experimental import pallas as pl
from jax.experimental.pallas import tpu as pltpu

F32 = jnp.float32
BF16 = jnp.bfloat16

D_MODEL = 1024
DEPTH = 2
BRANCH_W = 512
N_BRANCH = 3
GM_GROUPS = 4
GM_CHUNK = 128
HEADS = 4
HEAD_D = 128
RET_CHUNK = 128
RET_UNROLL = 2
GDN_CHUNK = 64
PHASE_A_CHUNKS = 4
CONV_TILE = 128
CONV_PAD = 8
ROPE_BASE = 10000.0
EPS = 1e-6
LANES = 128
MAIN_COLS = 11 * BRANCH_W
COL_GU, COL_GV, COL_GZ = 0, 1, 2
COL_RQ, COL_RK, COL_RV, COL_RZ = 3, 4, 5, 6
COL_DQ, COL_DK, COL_DV, COL_DZ = 7, 8, 9, 10

VMEM_LIMIT = 56 * 1024 * 1024


def _dot(a, b):
    return jnp.dot(a, b, preferred_element_type=F32)


def _dot_nt(a, b):
    return lax.dot_general(a, b, (((1,), (1,)), ((), ())), preferred_element_type=F32)


def _dot_tn(a, b):
    return lax.dot_general(a, b, (((0,), (0,)), ((), ())), preferred_element_type=F32)


def _silu(x):
    return x * jax.nn.sigmoid(x)


def _gelu(x):
    return 0.5 * x * (1.0 + lax.erf(x * 0.7071067811865476))


def _params(*sem):
    return pltpu.CompilerParams(dimension_semantics=sem, vmem_limit_bytes=VMEM_LIMIT)


def _head_slice(h):
    return slice(h * HEAD_D, (h + 1) * HEAD_D)


def _rows(start, size):
    return pl.ds(pl.multiple_of(start, size), size)


def _inproj_body(x_ref, g_ref, w_ref, wab_ref, o_ref, ab_ref, h_ref):
    @pl.when(pl.program_id(1) == 0)
    def _():
        x = x_ref[...]
        h = x * lax.rsqrt(jnp.mean(x * x, axis=-1, keepdims=True) + EPS) * g_ref[...]
        hb = h.astype(BF16)
        h_ref[...] = hb
        ab_ref[...] = _dot(hb, wab_ref[...])

    o_ref[...] = _dot(h_ref[...], w_ref[...]).astype(o_ref.dtype)


def _in_proj(x, g, w_main, w_ab, tm=1024, tn=1408):
    tokens = x.shape[0]
    tm = min(tm, tokens)
    return pl.pallas_call(
        _inproj_body,
        grid=(tokens // tm, MAIN_COLS // tn),
        in_specs=[
            pl.BlockSpec((tm, D_MODEL), lambda i, j: (i, 0)),
            pl.BlockSpec((1, D_MODEL), lambda i, j: (0, 0)),
            pl.BlockSpec((D_MODEL, tn), lambda i, j: (0, j)),
            pl.BlockSpec((D_MODEL, LANES), lambda i, j: (0, 0)),
        ],
        out_specs=[
            pl.BlockSpec((tm, tn), lambda i, j: (i, j)),
            pl.BlockSpec((tm, LANES), lambda i, j: (i, 0)),
        ],
        out_shape=[
            jax.ShapeDtypeStruct((tokens, MAIN_COLS), BF16),
            jax.ShapeDtypeStruct((tokens, LANES), F32),
        ],
        scratch_shapes=[pltpu.VMEM((tm, D_MODEL), BF16)],
        compiler_params=_params("arbitrary", "arbitrary"),
        name="in_proj",
    )(x, g, w_main, w_ab)


def _gmlp_body(u_ref, v_ref, z_ref, lng_ref, lnb_ref, ws_ref, bias_ref, o_ref):
    rows = u_ref.shape[0]
    for c in range(rows // GM_CHUNK):
        r = slice(c * GM_CHUNK, (c + 1) * GM_CHUNK)
        u = _gelu(u_ref[r, :].astype(F32))
        v = _gelu(v_ref[r, :].astype(F32))
        z = z_ref[r, :].astype(F32)
        mu = jnp.mean(v, axis=-1, keepdims=True)
        vc = v - mu
        var = jnp.mean(vc * vc, axis=-1, keepdims=True)
        vn = (vc * lax.rsqrt(var + EPS) * lng_ref[...] + lnb_ref[...]).astype(BF16)
        gate = u * _silu(z)
        for g in range(GM_GROUPS):
            cs = slice(g * LANES, (g + 1) * LANES)
            mixed = _dot(ws_ref[g], vn[:, cs]) + bias_ref[:, cs]
            o_ref[r, cs] = (gate[:, cs] * mixed).astype(o_ref.dtype)


def _gmlp(proj, ln_g, ln_b, w_s, bias, tm=512):
    tokens = proj.shape[0]
    col = lambda c: pl.BlockSpec((tm, BRANCH_W), lambda i, c=c: (i, c))
    full = lambda shape: pl.BlockSpec(shape, lambda i: (0,) * len(shape))
    return pl.pallas_call(
        _gmlp_body,
        grid=(tokens // tm,),
        in_specs=[col(COL_GU), col(COL_GV), col(COL_GZ), full((1, BRANCH_W)), full((1, BRANCH_W)),
                  full((GM_GROUPS, GM_CHUNK, GM_CHUNK)), full((GM_CHUNK, BRANCH_W))],
        out_specs=pl.BlockSpec((tm, BRANCH_W), lambda i: (i, 0)),
        out_shape=jax.ShapeDtypeStruct((tokens, BRANCH_W), BF16),
        compiler_params=_params("arbitrary"),
        name="gmlp",
    )(proj, proj, proj, ln_g, ln_b, w_s, bias)


def _ret_body(q_ref, k_ref, v_ref, z_ref, cos_ref, sin_ref, dl_ref, ng_ref, y_ref, qr_s, kr_s, kv_s):
    C = RET_CHUNK
    n_chunks = q_ref.shape[0] // C
    row = lax.broadcasted_iota(jnp.int32, (C, C), 0).astype(F32)
    colm = lax.broadcasted_iota(jnp.int32, (C, C), 1).astype(F32)
    rel = row - colm

    decs = []
    for h in range(HEADS):
        lf = jnp.broadcast_to(jax.nn.log_sigmoid(dl_ref[0, h])[0:1, :], (C, C))
        lb = jnp.broadcast_to(jax.nn.log_sigmoid(dl_ref[1, h])[0:1, :], (C, C))
        decs.append(dict(
            mask=jnp.exp(jnp.where(rel >= 0, lf * rel, -lb * rel)),
            qdf=jnp.exp(lf * (row + 1.0)), kdf=jnp.exp(lf * (C - 1.0 - row)),
            qdb=jnp.exp(lb * (C - row)), kdb=jnp.exp(lb * row),
            cf=jnp.exp(lf * C), cb=jnp.exp(lb * C)))

    def phase1(it, carry):
        cs = [it * RET_UNROLL + t for t in range(RET_UNROLL)]
        rows = [_rows(c * C, C) for c in cs]
        items = [(t, h) for t in range(RET_UNROLL) for h in range(HEADS)]
        cos = [cos_ref[r, :] for r in rows]
        sin = [sin_ref[r, :] for r in rows]
        qs = [q_ref[rows[t], _head_slice(h)].astype(F32) for t, h in items]
        ks = [k_ref[rows[t], _head_slice(h)].astype(F32) for t, h in items]
        qr = [q * cos[t] + pltpu.roll(q, HEAD_D // 2, 1) * sin[t] for (t, h), q in zip(items, qs)]
        kr = [(k * cos[t] + pltpu.roll(k, HEAD_D // 2, 1) * sin[t]) * (HEAD_D ** -0.5) for (t, h), k in zip(items, ks)]
        kcat = [jnp.concatenate([k * decs[h]["kdf"], k * decs[h]["kdb"]], axis=1).astype(BF16)
                for (t, h), k in zip(items, kr)]
        kvs = [_dot_tn(kc, v_ref[rows[t], _head_slice(h)]) for (t, h), kc in zip(items, kcat)]
        for n, (t, h) in enumerate(items):
            qr_s[rows[t], _head_slice(h)] = qr[n].astype(qr_s.dtype)
            kr_s[rows[t], _head_slice(h)] = kr[n].astype(kr_s.dtype)
            kv_s[h, cs[t]] = kvs[n]
        return carry

    lax.fori_loop(0, n_chunks // RET_UNROLL, phase1, 0)

    for h in range(HEADS):
        cf, cb = decs[h]["cf"], decs[h]["cb"]

        def scan(i, carry, h=h, cf=cf, cb=cb):
            sf, sb = carry
            j = n_chunks - 1 - i
            upd_f = kv_s[h, i, 0:HEAD_D, :]
            kv_s[h, i, 0:HEAD_D, :] = sf
            upd_b = kv_s[h, j, HEAD_D:2 * HEAD_D, :]
            kv_s[h, j, HEAD_D:2 * HEAD_D, :] = sb
            return sf * cf + upd_f, sb * cb + upd_b

        zero = jnp.zeros((HEAD_D, HEAD_D), F32)
        lax.fori_loop(0, n_chunks, scan, (zero, zero))

    def phase3(it, carry):
        cs = [it * RET_UNROLL + t for t in range(RET_UNROLL)]
        rows = [_rows(c * C, C) for c in cs]
        items = [(t, h) for t in range(RET_UNROLL) for h in range(HEADS)]
        qr = [qr_s[rows[t], _head_slice(h)] for t, h in items]
        sc = [_dot_nt(q, kr_s[rows[t], _head_slice(h)]) * decs[h]["mask"] for (t, h), q in zip(items, qr)]
        lhs, rhs = [], []
        for n, (t, h) in enumerate(items):
            q = qr[n].astype(F32)
            lhs.append(jnp.concatenate([sc[n], q * decs[h]["qdf"], q * decs[h]["qdb"]], axis=1).astype(BF16))
            rhs.append(jnp.concatenate([v_ref[rows[t], _head_slice(h)], kv_s[h, cs[t]].astype(BF16)], axis=0))
        outs = [_dot(a, b) for a, b in zip(lhs, rhs)]
        mus = [jnp.mean(o, axis=-1, keepdims=True) for o in outs]
        ocs = [o - mu for o, mu in zip(outs, mus)]
        var = [jnp.mean(oc * oc, axis=-1, keepdims=True) for oc in ocs]
        for n, (t, h) in enumerate(items):
            hs = _head_slice(h)
            on = ocs[n] * lax.rsqrt(var[n] + EPS) * ng_ref[:, hs]
            y_ref[rows[t], hs] = (on * _silu(z_ref[rows[t], hs].astype(F32))).astype(y_ref.dtype)
        return carry

    lax.fori_loop(0, n_chunks // RET_UNROLL, phase3, 0)


def _retention(proj, cos, sin, dl, ng):
    seq = cos.shape[0]
    batch = proj.shape[0] // seq
    col = lambda c: pl.BlockSpec((seq, BRANCH_W), lambda b, c=c: (b, c))
    full = lambda shape: pl.BlockSpec(shape, lambda b: (0,) * len(shape))
    return pl.pallas_call(
        _ret_body,
        grid=(batch,),
        in_specs=[col(COL_RQ), col(COL_RK), col(COL_RV), col(COL_RZ),
                  full((seq, HEAD_D)), full((seq, HEAD_D)), full((2, HEADS, 8, LANES)), full((1, BRANCH_W))],
        out_specs=pl.BlockSpec((seq, BRANCH_W), lambda b: (b, 0)),
        out_shape=jax.ShapeDtypeStruct((batch * seq, BRANCH_W), BF16),
        scratch_shapes=[pltpu.VMEM((seq, BRANCH_W), BF16), pltpu.VMEM((seq, BRANCH_W), BF16),
                        pltpu.VMEM((HEADS, seq // RET_CHUNK, 2 * HEAD_D, HEAD_D), F32)],
        compiler_params=_params("arbitrary"),
        name="retention",
    )(proj, proj, proj, proj, cos, sin, dl, ng)


def _gdn_body(q_ref, k_ref, v_ref, z_ref, ab_ref, cw_ref, alog_ref, dtb_ref, ng_ref, y_ref,
              xp_s, qn_s, kn_s, vn_s, g_s, gcf_s, gcb_s, u_s, w_s, sc_s, of_s, ob_s, st_s):
    C = GDN_CHUNK
    seq = q_ref.shape[0]
    n_chunks = seq // C

    zpad = jnp.zeros((CONV_PAD, HEAD_D), F32)
    for h in range(HEADS):
        xp_s[h, 0:CONV_PAD, :] = zpad
        xp_s[h, seq + CONV_PAD:seq + 2 * CONV_PAD, :] = zpad
    for x_ref, dst, wofs, scale in ((q_ref, qn_s, 0, HEAD_D ** -0.5), (k_ref, kn_s, BRANCH_W, 1.0),
                                    (v_ref, vn_s, 2 * BRANCH_W, None)):
        def widen(i, carry, x_ref=x_ref):
            for h in range(HEADS):
                xp_s[h, pl.ds(pl.multiple_of(i * CONV_TILE + CONV_PAD, CONV_PAD), CONV_TILE), :] = (
                    x_ref[_rows(i * CONV_TILE, CONV_TILE), _head_slice(h)].astype(F32))
            return carry

        lax.fori_loop(0, seq // CONV_TILE, widen, 0)

        def conv_tile(i, carry, dst=dst, wofs=wofs, scale=scale):
            r0 = pl.multiple_of(i * CONV_TILE, CONV_TILE)
            for h in range(HEADS):
                hs = _head_slice(h)
                w = cw_ref[:, wofs + hs.start: wofs + hs.stop]
                y = _silu(xp_s[h, pl.ds(r0 + CONV_PAD - 1, CONV_TILE), :] * w[0:1, :]
                          + xp_s[h, pl.ds(r0 + CONV_PAD, CONV_TILE), :] * w[1:2, :]
                          + xp_s[h, pl.ds(r0 + CONV_PAD + 1, CONV_TILE), :] * w[2:3, :])
                if scale is not None:
                    y = y * (lax.rsqrt(jnp.sum(y * y, axis=-1, keepdims=True) + EPS) * scale)
                dst[pl.ds(r0, CONV_TILE), hs] = y.astype(dst.dtype)
            return carry

        lax.fori_loop(0, seq // CONV_TILE, conv_tile, 0)

    srow = lax.broadcasted_iota(jnp.int32, (seq, LANES), 0)
    ab = ab_ref[...]
    lane = lax.broadcasted_iota(jnp.int32, (seq, LANES), 1)
    g_all = jnp.where((lane & 7) < HEADS,
                      -jnp.exp(alog_ref[...]) * jax.nn.softplus(ab + dtb_ref[...]),
                      jax.nn.sigmoid(ab))
    g_s[...] = g_all
    pos = srow & (C - 1)
    gf = g_all
    gb = g_all
    s = 1
    while s < C:
        gf = gf + jnp.where(pos >= s, pltpu.roll(gf, s, 0), 0.0)
        gb = gb + jnp.where(pos < C - s, pltpu.roll(gb, seq - s, 0), 0.0)
        s *= 2
    gcf_s[...] = gf
    gcb_s[...] = gb

    ri = lax.broadcasted_iota(jnp.int32, (C, C), 0)
    ci = lax.broadcasted_iota(jnp.int32, (C, C), 1)
    eye = (ri == ci).astype(F32)
    masks = ((ri >= ci, ri > ci), (ri <= ci, ri < ci))
    levels = [((ri >> (j + 1)) == (ci >> (j + 1))) & ((ri >> j) != (ci >> j)) for j in range(6)]
    chains = [(d, h) for d in range(2) for h in range(HEADS)]
    gc_refs = (gcf_s, gcb_s)

    def phase_a(it, carry):
        rows = [_rows((it * PHASE_A_CHUNKS + t) * C, C) for t in range(PHASE_A_CHUNKS)]
        work = [(t, d, h) for t in range(PHASE_A_CHUNKS) for d, h in chains]
        gcs = {(t, d): gc_refs[d][rows[t], :] for t in range(PHASE_A_CHUNKS) for d in range(2)}
        gts = {key: g.T for key, g in gcs.items()}
        gbeta = [g_s[r, :] for r in rows]
        kb16 = {(t, h): kn_s[rows[t], _head_slice(h)] for t in range(PHASE_A_CHUNKS) for h in range(HEADS)}
        kks = {(t, h): _dot_nt(jnp.concatenate([kb16[t, h], qn_s[rows[t], _head_slice(h)]], axis=0), kb16[t, h])
               for t in range(PHASE_A_CHUNKS) for h in range(HEADS)}
        tri, rhs = [], []
        for t, d, h in work:
            incl, strict = masks[d]
            gl, bl = 8 * d + h, 8 * d + HEADS + h
            gcol = jnp.broadcast_to(gcs[t, d][:, gl:gl + 1], (C, HEAD_D))
            beta = jnp.broadcast_to(gbeta[t][:, bl:bl + 1], (C, HEAD_D))
            grow = jnp.broadcast_to(gts[t, d][gl:gl + 1, :], (C, C))
            decay = jnp.exp(jnp.where(incl, gcol[:, :C] - grow, -jnp.inf))
            kk = kks[t, h]
            tri.append(jnp.where(strict, beta[:, :C] * kk[:C] * decay, 0.0))
            sc_s[d, h, rows[t], :] = (kk[C:] * decay).astype(sc_s.dtype)
            kbeta = kb16[t, h].astype(F32) * beta
            rhs.append(jnp.concatenate([vn_s[rows[t], _head_slice(h)].astype(F32) * beta, kbeta * jnp.exp(gcol)],
                                       axis=1).astype(BF16))
        inv = [eye - jnp.where(levels[0], x, 0.0) for x in tri]
        for j in range(1, 6):
            m16 = [jnp.where(levels[j], x, 0.0).astype(BF16) for x in tri]
            i16 = [p.astype(BF16) for p in inv]
            mi = [_dot(m, p).astype(BF16) for m, p in zip(m16, i16)]
            inv = [p - _dot(p16, x) for p, p16, x in zip(inv, i16, mi)]
        for (t, d, h), p, b in zip(work, inv, rhs):
            sol = _dot(p.astype(BF16), b)
            u_s[d, rows[t], _head_slice(h)] = sol[:, :HEAD_D].astype(u_s.dtype)
            w_s[d, rows[t], _head_slice(h)] = sol[:, HEAD_D:].astype(w_s.dtype)
        return carry

    lax.fori_loop(0, n_chunks // PHASE_A_CHUNKS, phase_a, 0)

    st_s[...] = jnp.zeros_like(st_s)
    o_refs = (of_s, ob_s)

    def phase_b(i, carry):
        rs = [_rows(c * C, C) for c in (i, n_chunks - 1 - i)]
        gcs = [gc_refs[d][rs[d], :] for d in range(2)]
        lhs, kd, egl = [], [], []
        for d, h in chains:
            r, hs = rs[d], _head_slice(h)
            gl = 8 * d + h
            gcol = jnp.broadcast_to(gcs[d][:, gl:gl + 1], (C, HEAD_D))
            last = C - 1 if d == 0 else 0
            glast = gcol[last:last + 1, :]
            qe = (qn_s[r, hs].astype(F32) * jnp.exp(gcol)).astype(BF16)
            lhs.append(jnp.concatenate([w_s[d, r, hs], qe], axis=0))
            kd.append((kn_s[r, hs].astype(F32) * jnp.exp(glast - gcol)).astype(BF16))
            egl.append(jnp.exp(glast))
        st = [st_s[n] for n in range(len(chains))]
        ws = [_dot(a, s_.astype(BF16)) for a, s_ in zip(lhs, st)]
        vnew = [(u_s[d, rs[d], _head_slice(h)].astype(F32) - x[:C]).astype(BF16) for (d, h), x in zip(chains, ws)]
        for n, (d, h) in enumerate(chains):
            o_refs[d][rs[d], _head_slice(h)] = (ws[n][C:] + _dot(sc_s[d, h, rs[d], :], vnew[n])).astype(of_s.dtype)
            st_s[n] = st[n] * egl[n] + _dot_tn(kd[n], vnew[n])
        return carry

    lax.fori_loop(0, n_chunks, phase_b, 0)

    def finish(i, carry):
        r = _rows(i * 2 * C, 2 * C)
        for h in range(HEADS):
            hs = _head_slice(h)
            o = of_s[r, hs].astype(F32) + ob_s[r, hs].astype(F32)
            on = o * lax.rsqrt(jnp.mean(o * o, axis=-1, keepdims=True) + EPS) * ng_ref[...]
            y_ref[r, hs] = (on * _silu(z_ref[r, hs].astype(F32))).astype(y_ref.dtype)
        return carry

    lax.fori_loop(0, n_chunks // 2, finish, 0)


def _gdn(proj, ab, conv_w, alog_row, dtb_row, ng, seq):
    batch = proj.shape[0] // seq
    col = lambda c: pl.BlockSpec((seq, BRANCH_W), lambda b, c=c: (b, c), pipeline_mode=pl.Buffered(1))
    full = lambda shape: pl.BlockSpec(shape, lambda b: (0,) * len(shape))
    big = lambda dt: pltpu.VMEM((seq, BRANCH_W), dt)
    small = pltpu.VMEM((seq, LANES), F32)
    return pl.pallas_call(
        _gdn_body,
        grid=(batch,),
        in_specs=[col(COL_DQ), col(COL_DK), col(COL_DV), col(COL_DZ),
                  pl.BlockSpec((seq, LANES), lambda b: (b, 0)),
                  full((3, 3 * BRANCH_W)), full((1, LANES)), full((1, LANES)), full((1, HEAD_D))],
        out_specs=pl.BlockSpec((seq, BRANCH_W), lambda b: (b, 0)),
        out_shape=jax.ShapeDtypeStruct((batch * seq, BRANCH_W), BF16),
        scratch_shapes=[pltpu.VMEM((HEADS, seq + 2 * CONV_PAD, HEAD_D), F32),
                        big(BF16), big(BF16), big(BF16), small, small, small,
                        pltpu.VMEM((2, seq, BRANCH_W), BF16), pltpu.VMEM((2, seq, BRANCH_W), BF16),
                        pltpu.VMEM((2, HEADS, seq, GDN_CHUNK), BF16),
                        big(BF16), big(BF16), pltpu.VMEM((2 * HEADS, HEAD_D, HEAD_D), F32)],
        compiler_params=_params("arbitrary"),
        name="gdn",
    )(proj, proj, proj, proj, ab, conv_w, alog_row, dtb_row, ng)


def _merge_body(final, x_ref, g_ref, ya_ref, yb_ref, yc_ref, wg_ref, wbo_ref, wo_ref, *rest):
    if final:
        fg_ref, o_ref = rest
    else:
        (o_ref,) = rest
    x = x_ref[...]
    h = (x * lax.rsqrt(jnp.mean(x * x, axis=-1, keepdims=True) + EPS) * g_ref[...]).astype(BF16)
    merged = None
    for n, y_ref in enumerate((ya_ref, yb_ref, yc_ref)):
        gate = jax.nn.sigmoid(_dot(h, wg_ref[n]))
        term = gate * _dot(y_ref[...], wbo_ref[n])
        merged = term if merged is None else merged + term
    out = x + _dot(merged.astype(BF16), wo_ref[...])
    if final:
        out = out * lax.rsqrt(jnp.mean(out * out, axis=-1, keepdims=True) + EPS) * fg_ref[...]
    o_ref[...] = out


def _merge(x, g, ya, yb, yc, wg, wbo, wo, final_g=None, tm=512):
    tokens = x.shape[0]
    final = final_g is not None
    row = lambda w: pl.BlockSpec((tm, w), lambda i: (i, 0))
    full = lambda shape: pl.BlockSpec(shape, lambda i: (0,) * len(shape))
    in_specs = [row(D_MODEL), full((1, D_MODEL)), row(BRANCH_W), row(BRANCH_W), row(BRANCH_W),
                full((N_BRANCH, D_MODEL, D_MODEL)), full((N_BRANCH, BRANCH_W, D_MODEL)), full((D_MODEL, D_MODEL))]
    args = [x, g, ya, yb, yc, wg, wbo, wo]
    if final:
        in_specs.append(full((1, D_MODEL)))
        args.append(final_g)
    return pl.pallas_call(
        functools.partial(_merge_body, final),
        grid=(tokens // tm,),
        in_specs=in_specs,
        out_specs=row(D_MODEL),
        out_shape=jax.ShapeDtypeStruct((tokens, D_MODEL), F32),
        compiler_params=_params("arbitrary"),
        name="merge_final" if final else "merge",
    )(*args)


def _rope_tables(seq):
    inv = ROPE_BASE ** (-jnp.arange(0, HEAD_D, 2, dtype=F32) / HEAD_D)
    ang = jnp.arange(seq, dtype=F32)[:, None] * inv[None, :]
    cos = jnp.cos(ang)
    sin = jnp.sin(ang)
    return jnp.concatenate([cos, cos], axis=-1), jnp.concatenate([-sin, sin], axis=-1)


def _lane_row(p):
    row = jnp.zeros((LANES,), F32)
    row = row.at[0:HEADS].set(p[0]).at[2 * HEADS:3 * HEADS].set(p[1])
    return row[None, :]


def kernel(x, norm_g, w_in, gm_ln_g, gm_ln_b, gm_w_s, gm_b_s, ret_decay_logit, ret_norm_g, gdn_conv_w, gdn_a_log, gdn_dt_bias, gdn_norm_g, w_gate, w_branch_out, w_out, final_norm_g):
    batch, seq, _ = x.shape
    xf = x.reshape(batch * seq, D_MODEL)
    cos, sin = _rope_tables(seq)
    for l in range(DEPTH):
        w_main = w_in[l][:, :MAIN_COLS].astype(BF16)
        w_ab = jnp.pad(w_in[l][:, MAIN_COLS:], ((0, 0), (0, LANES - 4 * HEADS))).astype(BF16)
        g_row = norm_g[l][None, :]
        proj, ab = _in_proj(xf, g_row, w_main, w_ab)

        bias = jnp.repeat(gm_b_s[l].T, GM_CHUNK, axis=1)
        ya = _gmlp(proj, gm_ln_g[l][None, :], gm_ln_b[l][None, :], gm_w_s[l].astype(BF16), bias)

        dl = jnp.broadcast_to(ret_decay_logit[l][:, :, None, None], (2, HEADS, 8, LANES))
        yb = _retention(proj, cos, sin, dl, ret_norm_g[l][None, :])

        yc = _gdn(proj, ab, gdn_conv_w[l], _lane_row(gdn_a_log[l]), _lane_row(gdn_dt_bias[l]),
                  gdn_norm_g[l][None, :], seq)

        wg = jnp.transpose(w_gate[l], (1, 0, 2)).astype(BF16)
        xf = _merge(xf, g_row, ya, yb, yc, wg, w_branch_out[l].astype(BF16), w_out[l].astype(BF16),
                    final_norm_g[None, :] if l == DEPTH - 1 else None)
    return xf.reshape(batch, seq, D_MODEL)
```

```python
import functools

import jax
import jax.numpy as jnp
from jax import lax
from jax.experimental import pallas as pl
from jax.experimental.pallas import tpu as pltpu

F32 = jnp.float32
BF16 = jnp.bfloat16

D_MODEL = 1024
DEPTH = 2
BRANCH_W = 512
N_BRANCH = 3
GM_GROUPS = 4
GM_CHUNK = 128
HEADS = 4
HEAD_D = 128
RET_CHUNK = 128
RET_UNROLL = 2
GDN_CHUNK = 64
PHASE_A_CHUNKS = 4
CONV_TILE = 128
CONV_PAD = 8
ROPE_BASE = 10000.0
EPS = 1e-6
LANES = 128
MAIN_COLS = 11 * BRANCH_W
GM_COLS = 3 * BRANCH_W
MIX_COLS = MAIN_COLS - GM_COLS
COL_RQ, COL_RK, COL_RV, COL_RZ = 0, 1, 2, 3
COL_DQ, COL_DK, COL_DV, COL_DZ = 4, 5, 6, 7

VMEM_LIMIT = 56 * 1024 * 1024


def _dot(a, b):
    return jnp.dot(a, b, preferred_element_type=F32)


def _dot_nt(a, b):
    return lax.dot_general(a, b, (((1,), (1,)), ((), ())), preferred_element_type=F32)


def _dot_tn(a, b):
    return lax.dot_general(a, b, (((0,), (0,)), ((), ())), preferred_element_type=F32)


def _silu(x):
    return x * jax.nn.sigmoid(x)


def _gelu(x):
    return 0.5 * x * (1.0 + lax.erf(x * 0.7071067811865476))


def _params(*sem):
    return pltpu.CompilerParams(dimension_semantics=sem, vmem_limit_bytes=VMEM_LIMIT)


def _head_slice(h):
    return slice(h * HEAD_D, (h + 1) * HEAD_D)


def _rows(start, size):
    return pl.ds(start if isinstance(start, int) else pl.multiple_of(start, size), size)


def _inproj_body(x_ref, g_ref, w_ref, wab_ref, lng_ref, lnb_ref, ws_ref, bias_ref, o_ref, ab_ref, ya_ref):
    x = x_ref[...]
    hb = (x * lax.rsqrt(jnp.mean(x * x, axis=-1, keepdims=True) + EPS) * g_ref[...]).astype(BF16)
    ab_ref[...] = _dot(hb, wab_ref[...])
    gm = _dot(hb, w_ref[:, 0:GM_COLS])
    o_ref[...] = _dot(hb, w_ref[:, GM_COLS:MAIN_COLS]).astype(o_ref.dtype)
    for c in range(x.shape[0] // GM_CHUNK):
        r = slice(c * GM_CHUNK, (c + 1) * GM_CHUNK)
        u = _gelu(gm[r, 0:BRANCH_W])
        v = _gelu(gm[r, BRANCH_W:2 * BRANCH_W])
        z = gm[r, 2 * BRANCH_W:3 * BRANCH_W]
        mu = jnp.mean(v, axis=-1, keepdims=True)
        vc = v - mu
        var = jnp.mean(vc * vc, axis=-1, keepdims=True)
        vn = (vc * lax.rsqrt(var + EPS) * lng_ref[...] + lnb_ref[...]).astype(BF16)
        gate = u * _silu(z)
        for g in range(GM_GROUPS):
            cs = slice(g * LANES, (g + 1) * LANES)
            mixed = _dot(ws_ref[g], vn[:, cs]) + bias_ref[:, cs]
            ya_ref[r, cs] = (gate[:, cs] * mixed).astype(ya_ref.dtype)


def _in_proj(x, g, w_main, w_ab, ln_g, ln_b, w_s, bias, tm=512):
    tokens = x.shape[0]
    row = lambda w: pl.BlockSpec((tm, w), lambda i: (i, 0))
    full = lambda shape: pl.BlockSpec(shape, lambda i: (0,) * len(shape))
    return pl.pallas_call(
        _inproj_body,
        grid=(tokens // tm,),
        in_specs=[row(D_MODEL), full((1, D_MODEL)), full((D_MODEL, MAIN_COLS)), full((D_MODEL, LANES)),
                  full((1, BRANCH_W)), full((1, BRANCH_W)), full((GM_GROUPS, GM_CHUNK, GM_CHUNK)),
                  full((GM_CHUNK, BRANCH_W))],
        out_specs=[row(MIX_COLS), row(LANES), row(BRANCH_W)],
        out_shape=[
            jax.ShapeDtypeStruct((tokens, MIX_COLS), BF16),
            jax.ShapeDtypeStruct((tokens, LANES), F32),
            jax.ShapeDtypeStruct((tokens, BRANCH_W), BF16),
        ],
        compiler_params=_params("arbitrary"),
        name="in_proj",
    )(x, g, w_main, w_ab, ln_g, ln_b, w_s, bias)


def _ret_body(q_ref, k_ref, v_ref, z_ref, cos_ref, sin_ref, dl_ref, ng_ref, y_ref, qr_s, kr_s, kv_s):
    C = RET_CHUNK
    n_chunks = q_ref.shape[0] // C
    row = lax.broadcasted_iota(jnp.int32, (C, C), 0).astype(F32)
    colm = lax.broadcasted_iota(jnp.int32, (C, C), 1).astype(F32)
    rel = row - colm

    decs = []
    for h in range(HEADS):
        lf = jnp.broadcast_to(jax.nn.log_sigmoid(dl_ref[0, h])[0:1, :], (C, C))
        lb = jnp.broadcast_to(jax.nn.log_sigmoid(dl_ref[1, h])[0:1, :], (C, C))
        decs.append(dict(
            mask=jnp.exp(jnp.where(rel >= 0, lf * rel, -lb * rel)),
            qdf=jnp.exp(lf * (row + 1.0)), kdf=jnp.exp(lf * (C - 1.0 - row)),
            qdb=jnp.exp(lb * (C - row)), kdb=jnp.exp(lb * row),
            cf=jnp.exp(lf * C), cb=jnp.exp(lb * C)))

    def phase1(it, carry):
        cs = [it * RET_UNROLL + t for t in range(RET_UNROLL)]
        rows = [_rows(c * C, C) for c in cs]
        items = [(t, h) for t in range(RET_UNROLL) for h in range(HEADS)]
        cos = [cos_ref[r, :] for r in rows]
        sin = [sin_ref[r, :] for r in rows]
        qs = [q_ref[rows[t], _head_slice(h)].astype(F32) for t, h in items]
        ks = [k_ref[rows[t], _head_slice(h)].astype(F32) for t, h in items]
        qr = [q * cos[t] + pltpu.roll(q, HEAD_D // 2, 1) * sin[t] for (t, h), q in zip(items, qs)]
        kr = [(k * cos[t] + pltpu.roll(k, HEAD_D // 2, 1) * sin[t]) * (HEAD_D ** -0.5) for (t, h), k in zip(items, ks)]
        kcat = [jnp.concatenate([k * decs[h]["kdf"], k * decs[h]["kdb"]], axis=1).astype(BF16)
                for (t, h), k in zip(items, kr)]
        kvs = [_dot_tn(kc, v_ref[rows[t], _head_slice(h)]) for (t, h), kc in zip(items, kcat)]
        for n, (t, h) in enumerate(items):
            qr_s[rows[t], _head_slice(h)] = qr[n].astype(qr_s.dtype)
            kr_s[rows[t], _head_slice(h)] = kr[n].astype(kr_s.dtype)
            kv_s[h, cs[t]] = kvs[n]
        return carry

    lax.fori_loop(0, n_chunks // RET_UNROLL, phase1, 0)

    for h in range(HEADS):
        cf, cb = decs[h]["cf"], decs[h]["cb"]

        def scan(i, carry, h=h, cf=cf, cb=cb):
            sf, sb = carry
            j = n_chunks - 1 - i
            upd_f = kv_s[h, i, 0:HEAD_D, :]
            kv_s[h, i, 0:HEAD_D, :] = sf
            upd_b = kv_s[h, j, HEAD_D:2 * HEAD_D, :]
            kv_s[h, j, HEAD_D:2 * HEAD_D, :] = sb
            return sf * cf + upd_f, sb * cb + upd_b

        zero = jnp.zeros((HEAD_D, HEAD_D), F32)
        lax.fori_loop(0, n_chunks, scan, (zero, zero))

    def phase3(it, carry):
        cs = [it * RET_UNROLL + t for t in range(RET_UNROLL)]
        rows = [_rows(c * C, C) for c in cs]
        items = [(t, h) for t in range(RET_UNROLL) for h in range(HEADS)]
        qr = [qr_s[rows[t], _head_slice(h)] for t, h in items]
        sc = [_dot_nt(q, kr_s[rows[t], _head_slice(h)]) * decs[h]["mask"] for (t, h), q in zip(items, qr)]
        lhs, rhs = [], []
        for n, (t, h) in enumerate(items):
            q = qr[n].astype(F32)
            lhs.append(jnp.concatenate([sc[n], q * decs[h]["qdf"], q * decs[h]["qdb"]], axis=1).astype(BF16))
            rhs.append(jnp.concatenate([v_ref[rows[t], _head_slice(h)], kv_s[h, cs[t]].astype(BF16)], axis=0))
        outs = [_dot(a, b) for a, b in zip(lhs, rhs)]
        mus = [jnp.mean(o, axis=-1, keepdims=True) for o in outs]
        ocs = [o - mu for o, mu in zip(outs, mus)]
        var = [jnp.mean(oc * oc, axis=-1, keepdims=True) for oc in ocs]
        for n, (t, h) in enumerate(items):
            hs = _head_slice(h)
            on = ocs[n] * lax.rsqrt(var[n] + EPS) * ng_ref[:, hs]
            y_ref[rows[t], hs] = (on * _silu(z_ref[rows[t], hs].astype(F32))).astype(y_ref.dtype)
        return carry

    lax.fori_loop(0, n_chunks // RET_UNROLL, phase3, 0)


def _retention(proj, cos, sin, dl, ng):
    seq = cos.shape[0]
    batch = proj.shape[0] // seq
    col = lambda c: pl.BlockSpec((seq, BRANCH_W), lambda b, c=c: (b, c))
    full = lambda shape: pl.BlockSpec(shape, lambda b: (0,) * len(shape))
    return pl.pallas_call(
        _ret_body,
        grid=(batch,),
        in_specs=[col(COL_RQ), col(COL_RK), col(COL_RV), col(COL_RZ),
                  full((seq, HEAD_D)), full((seq, HEAD_D)), full((2, HEADS, 8, LANES)), full((1, BRANCH_W))],
        out_specs=pl.BlockSpec((seq, BRANCH_W), lambda b: (b, 0)),
        out_shape=jax.ShapeDtypeStruct((batch * seq, BRANCH_W), BF16),
        scratch_shapes=[pltpu.VMEM((seq, BRANCH_W), BF16), pltpu.VMEM((seq, BRANCH_W), BF16),
                        pltpu.VMEM((HEADS, seq // RET_CHUNK, 2 * HEAD_D, HEAD_D), F32)],
        compiler_params=_params("arbitrary"),
        name="retention",
    )(proj, proj, proj, proj, cos, sin, dl, ng)


def _gdn_body(q_ref, k_ref, v_ref, z_ref, ab_ref, cw_ref, alog_ref, dtb_ref, ng_ref, y_ref,
              xp_s, qn_s, kn_s, vn_s, g_s, gcf_s, gcb_s, u_s, w_s, sc_s, of_s, ob_s, st_s):
    C = GDN_CHUNK
    seq = q_ref.shape[0]
    n_chunks = seq // C

    zpad = jnp.zeros((CONV_PAD, HEAD_D), F32)
    for h in range(HEADS):
        xp_s[h, 0:CONV_PAD, :] = zpad
        xp_s[h, seq + CONV_PAD:seq + 2 * CONV_PAD, :] = zpad
    for x_ref, dst, wofs, scale in ((q_ref, qn_s, 0, HEAD_D ** -0.5), (k_ref, kn_s, BRANCH_W, 1.0),
                                    (v_ref, vn_s, 2 * BRANCH_W, None)):
        def widen(i, carry, x_ref=x_ref):
            for h in range(HEADS):
                xp_s[h, pl.ds(pl.multiple_of(i * CONV_TILE + CONV_PAD, CONV_PAD), CONV_TILE), :] = (
                    x_ref[_rows(i * CONV_TILE, CONV_TILE), _head_slice(h)].astype(F32))
            return carry

        lax.fori_loop(0, seq // CONV_TILE, widen, 0)

        def conv_tile(i, carry, dst=dst, wofs=wofs, scale=scale):
            r0 = pl.multiple_of(i * CONV_TILE, CONV_TILE)
            for h in range(HEADS):
                hs = _head_slice(h)
                w = cw_ref[:, wofs + hs.start: wofs + hs.stop]
                y = _silu(xp_s[h, pl.ds(r0 + CONV_PAD - 1, CONV_TILE), :] * w[0:1, :]
                          + xp_s[h, pl.ds(r0 + CONV_PAD, CONV_TILE), :] * w[1:2, :]
                          + xp_s[h, pl.ds(r0 + CONV_PAD + 1, CONV_TILE), :] * w[2:3, :])
                if scale is not None:
                    y = y * (lax.rsqrt(jnp.sum(y * y, axis=-1, keepdims=True) + EPS) * scale)
                dst[pl.ds(r0, CONV_TILE), hs] = y.astype(dst.dtype)
            return carry

        lax.fori_loop(0, seq // CONV_TILE, conv_tile, 0)

    srow = lax.broadcasted_iota(jnp.int32, (seq, LANES), 0)
    ab = ab_ref[...]
    lane = lax.broadcasted_iota(jnp.int32, (seq, LANES), 1)
    g_all = jnp.where((lane & 7) < HEADS,
                      -jnp.exp(alog_ref[...]) * jax.nn.softplus(ab + dtb_ref[...]),
                      jax.nn.sigmoid(ab))
    g_s[...] = g_all
    pos = srow & (C - 1)
    gf = g_all
    gb = g_all
    s = 1
    while s < C:
        gf = gf + jnp.where(pos >= s, pltpu.roll(gf, s, 0), 0.0)
        gb = gb + jnp.where(pos < C - s, pltpu.roll(gb, seq - s, 0), 0.0)
        s *= 2
    gcf_s[...] = gf
    gcb_s[...] = gb

    ri = lax.broadcasted_iota(jnp.int32, (C, C), 0)
    ci = lax.broadcasted_iota(jnp.int32, (C, C), 1)
    eye = (ri == ci).astype(F32)
    masks = ((ri >= ci, ri > ci), (ri <= ci, ri < ci))
    levels = [((ri >> (j + 1)) == (ci >> (j + 1))) & ((ri >> j) != (ci >> j)) for j in range(6)]
    chains = [(d, h) for d in range(2) for h in range(HEADS)]
    gc_refs = (gcf_s, gcb_s)

    U = PHASE_A_CHUNKS
    n_groups = n_chunks // U

    def group_rows(grp):
        return ([_rows((grp * U + t) * C, C) for t in range(U)],
                [_rows((n_chunks - 1 - grp * U - t) * C, C) for t in range(U)])

    def part_a(grp):
        rows = group_rows(grp)
        work = [(t, d, h) for t in range(U) for d, h in chains]
        gcs = {(t, d): gc_refs[d][rows[d][t], :] for t in range(U) for d in range(2)}
        gts = {key: g.T for key, g in gcs.items()}
        gbeta = {(t, d): g_s[rows[d][t], :] for t in range(U) for d in range(2)}
        tri, rhs = [], []
        for t, d, h in work:
            r, hs = rows[d][t], _head_slice(h)
            incl, strict = masks[d]
            gl, bl = 8 * d + h, 8 * d + HEADS + h
            k16 = kn_s[r, hs]
            kk = _dot_nt(jnp.concatenate([k16, qn_s[r, hs]], axis=0), k16)
            gcol = jnp.broadcast_to(gcs[t, d][:, gl:gl + 1], (C, HEAD_D))
            beta = jnp.broadcast_to(gbeta[t, d][:, bl:bl + 1], (C, HEAD_D))
            grow = jnp.broadcast_to(gts[t, d][gl:gl + 1, :], (C, C))
            decay = jnp.exp(jnp.where(incl, gcol[:, :C] - grow, -jnp.inf))
            tri.append(jnp.where(strict, beta[:, :C] * kk[:C] * decay, 0.0))
            sc_s[d, h, r, :] = (kk[C:] * decay).astype(sc_s.dtype)
            kbeta = k16.astype(F32) * beta
            rhs.append(jnp.concatenate([vn_s[r, hs].astype(F32) * beta, kbeta * jnp.exp(gcol)],
                                       axis=1).astype(BF16))
        inv = [eye - jnp.where(levels[0], x, 0.0) for x in tri]
        for j in range(1, 6):
            m16 = [jnp.where(levels[j], x, 0.0).astype(BF16) for x in tri]
            i16 = [p.astype(BF16) for p in inv]
            mi = [_dot(m, p).astype(BF16) for m, p in zip(m16, i16)]
            inv = [p - _dot(p16, x) for p, p16, x in zip(inv, i16, mi)]
        for (t, d, h), p, b in zip(work, inv, rhs):
            sol = _dot(p.astype(BF16), b)
            u_s[d, rows[d][t], _head_slice(h)] = sol[:, :HEAD_D].astype(u_s.dtype)
            w_s[d, rows[d][t], _head_slice(h)] = sol[:, HEAD_D:].astype(w_s.dtype)

    o_refs = (of_s, ob_s)

    def part_b(grp):
        rows = group_rows(grp)
        st = [st_s[n] for n in range(len(chains))]
        for t in range(U):
            gcs = [gc_refs[d][rows[d][t], :] for d in range(2)]
            lhs, kd, egl = [], [], []
            for d, h in chains:
                r, hs = rows[d][t], _head_slice(h)
                gl = 8 * d + h
                gcol = jnp.broadcast_to(gcs[d][:, gl:gl + 1], (C, HEAD_D))
                last = C - 1 if d == 0 else 0
                glast = gcol[last:last + 1, :]
                qe = (qn_s[r, hs].astype(F32) * jnp.exp(gcol)).astype(BF16)
                lhs.append(jnp.concatenate([w_s[d, r, hs], qe], axis=0))
                kd.append((kn_s[r, hs].astype(F32) * jnp.exp(glast - gcol)).astype(BF16))
                egl.append(jnp.exp(glast))
            ws = [_dot(a, s_.astype(BF16)) for a, s_ in zip(lhs, st)]
            vnew = [(u_s[d, rows[d][t], _head_slice(h)].astype(F32) - x[:C]).astype(BF16)
                    for (d, h), x in zip(chains, ws)]
            for n, (d, h) in enumerate(chains):
                o_refs[d][rows[d][t], _head_slice(h)] = (
                    ws[n][C:] + _dot(sc_s[d, h, rows[d][t], :], vnew[n])).astype(of_s.dtype)
            st = [s_ * e + _dot_tn(a, x) for s_, e, a, x in zip(st, egl, kd, vnew)]
        for n, s_ in enumerate(st):
            st_s[n] = s_

    st_s[...] = jnp.zeros_like(st_s)
    part_a(0)

    def fused(grp, carry):
        part_b(grp - 1)
        part_a(grp)
        return carry

    lax.fori_loop(1, n_groups, fused, 0)
    part_b(n_groups - 1)

    def finish(i, carry):
        r = _rows(i * 2 * C, 2 * C)
        for h in range(HEADS):
            hs = _head_slice(h)
            o = of_s[r, hs].astype(F32) + ob_s[r, hs].astype(F32)
            on = o * lax.rsqrt(jnp.mean(o * o, axis=-1, keepdims=True) + EPS) * ng_ref[...]
            y_ref[r, hs] = (on * _silu(z_ref[r, hs].astype(F32))).astype(y_ref.dtype)
        return carry

    lax.fori_loop(0, n_chunks // 2, finish, 0)


def _gdn(proj, ab, conv_w, alog_row, dtb_row, ng, seq):
    batch = proj.shape[0] // seq
    col = lambda c: pl.BlockSpec((seq, BRANCH_W), lambda b, c=c: (b, c), pipeline_mode=pl.Buffered(1))
    full = lambda shape: pl.BlockSpec(shape, lambda b: (0,) * len(shape))
    big = lambda dt: pltpu.VMEM((seq, BRANCH_W), dt)
    small = pltpu.VMEM((seq, LANES), F32)
    return pl.pallas_call(
        _gdn_body,
        grid=(batch,),
        in_specs=[col(COL_DQ), col(COL_DK), col(COL_DV), col(COL_DZ),
                  pl.BlockSpec((seq, LANES), lambda b: (b, 0)),
                  full((3, 3 * BRANCH_W)), full((1, LANES)), full((1, LANES)), full((1, HEAD_D))],
        out_specs=pl.BlockSpec((seq, BRANCH_W), lambda b: (b, 0)),
        out_shape=jax.ShapeDtypeStruct((batch * seq, BRANCH_W), BF16),
        scratch_shapes=[pltpu.VMEM((HEADS, seq + 2 * CONV_PAD, HEAD_D), F32),
                        big(BF16), big(BF16), big(BF16), small, small, small,
                        pltpu.VMEM((2, seq, BRANCH_W), BF16), pltpu.VMEM((2, seq, BRANCH_W), BF16),
                        pltpu.VMEM((2, HEADS, seq, GDN_CHUNK), BF16),
                        big(BF16), big(BF16), pltpu.VMEM((2 * HEADS, HEAD_D, HEAD_D), F32)],
        compiler_params=_params("arbitrary"),
        name="gdn",
    )(proj, proj, proj, proj, ab, conv_w, alog_row, dtb_row, ng)


def _merge_body(final, x_ref, g_ref, ya_ref, yb_ref, yc_ref, wg_ref, wbo_ref, wo_ref, *rest):
    if final:
        fg_ref, o_ref = rest
    else:
        (o_ref,) = rest
    x = x_ref[...]
    h = (x * lax.rsqrt(jnp.mean(x * x, axis=-1, keepdims=True) + EPS) * g_ref[...]).astype(BF16)
    merged = None
    for n, y_ref in enumerate((ya_ref, yb_ref, yc_ref)):
        gate = jax.nn.sigmoid(_dot(h, wg_ref[n]))
        term = gate * _dot(y_ref[...], wbo_ref[n])
        merged = term if merged is None else merged + term
    out = x + _dot(merged.astype(BF16), wo_ref[...])
    if final:
        out = out * lax.rsqrt(jnp.mean(out * out, axis=-1, keepdims=True) + EPS) * fg_ref[...]
    o_ref[...] = out


def _merge(x, g, ya, yb, yc, wg, wbo, wo, final_g=None, tm=512):
    tokens = x.shape[0]
    final = final_g is not None
    row = lambda w: pl.BlockSpec((tm, w), lambda i: (i, 0))
    full = lambda shape: pl.BlockSpec(shape, lambda i: (0,) * len(shape))
    in_specs = [row(D_MODEL), full((1, D_MODEL)), row(BRANCH_W), row(BRANCH_W), row(BRANCH_W),
                full((N_BRANCH, D_MODEL, D_MODEL)), full((N_BRANCH, BRANCH_W, D_MODEL)), full((D_MODEL, D_MODEL))]
    args = [x, g, ya, yb, yc, wg, wbo, wo]
    if final:
        in_specs.append(full((1, D_MODEL)))
        args.append(final_g)
    return pl.pallas_call(
        functools.partial(_merge_body, final),
        grid=(tokens // tm,),
        in_specs=in_specs,
        out_specs=row(D_MODEL),
        out_shape=jax.ShapeDtypeStruct((tokens, D_MODEL), F32),
        compiler_params=_params("arbitrary"),
        name="merge_final" if final else "merge",
    )(*args)


def _rope_tables(seq):
    inv = ROPE_BASE ** (-jnp.arange(0, HEAD_D, 2, dtype=F32) / HEAD_D)
    ang = jnp.arange(seq, dtype=F32)[:, None] * inv[None, :]
    cos = jnp.cos(ang)
    sin = jnp.sin(ang)
    return jnp.concatenate([cos, cos], axis=-1), jnp.concatenate([-sin, sin], axis=-1)


def _lane_row(p):
    row = jnp.zeros((LANES,), F32)
    row = row.at[0:HEADS].set(p[0]).at[2 * HEADS:3 * HEADS].set(p[1])
    return row[None, :]


def kernel(x, norm_g, w_in, gm_ln_g, gm_ln_b, gm_w_s, gm_b_s, ret_decay_logit, ret_norm_g, gdn_conv_w, gdn_a_log, gdn_dt_bias, gdn_norm_g, w_gate, w_branch_out, w_out, final_norm_g):
    batch, seq, _ = x.shape
    xf = x.reshape(batch * seq, D_MODEL)
    cos, sin = _rope_tables(seq)
    for l in range(DEPTH):
        w_main = w_in[l][:, :MAIN_COLS].astype(BF16)
        w_ab = jnp.pad(w_in[l][:, MAIN_COLS:], ((0, 0), (0, LANES - 4 * HEADS))).astype(BF16)
        g_row = norm_g[l][None, :]
        bias = jnp.repeat(gm_b_s[l].T, GM_CHUNK, axis=1)
        proj, ab, ya = _in_proj(xf, g_row, w_main, w_ab, gm_ln_g[l][None, :], gm_ln_b[l][None, :],
                                gm_w_s[l].astype(BF16), bias)

        dl = jnp.broadcast_to(ret_decay_logit[l][:, :, None, None], (2, HEADS, 8, LANES))
        yb = _retention(proj, cos, sin, dl, ret_norm_g[l][None, :])

        yc = _gdn(proj, ab, gdn_conv_w[l], _lane_row(gdn_a_log[l]), _lane_row(gdn_dt_bias[l]),
                  gdn_norm_g[l][None, :], seq)

        wg = jnp.transpose(w_gate[l], (1, 0, 2)).astype(BF16)
        xf = _merge(xf, g_row, ya, yb, yc, wg, w_branch_out[l].astype(BF16), w_out[l].astype(BF16),
                    final_norm_g[None, :] if l == DEPTH - 1 else None)
    return xf.reshape(batch, seq, D_MODEL)
```

```python
import functools

import jax
import jax.numpy as jnp
from jax import lax
from jax.experimental import pallas as pl
from jax.experimental.pallas import tpu as pltpu

F32 = jnp.float32
BF16 = jnp.bfloat16

D_MODEL = 1024
DEPTH = 2
BRANCH_W = 512
N_BRANCH = 3
GM_GROUPS = 4
GM_CHUNK = 128
HEADS = 4
HEAD_D = 128
RET_CHUNK = 128
RET_UNROLL = 4
GDN_CHUNK = 64
PHASE_A_CHUNKS = 4
CONV_TILE = 128
ROPE_BASE = 10000.0
EPS = 1e-6
LANES = 128
MAIN_COLS = 11 * BRANCH_W
GM_COLS = 3 * BRANCH_W
PLAIN_COLS = 5 * BRANCH_W
CONV_COLS = 3 * BRANCH_W
MIX_COLS = PLAIN_COLS + CONV_COLS
COL_RQ, COL_RK, COL_RV, COL_RZ = 0, 1, 2, 3
COL_DZ, COL_DQ, COL_DK, COL_DV = 4, 5, 6, 7
HALO = 8

VMEM_LIMIT = 56 * 1024 * 1024


def _dot(a, b):
    return jnp.dot(a, b, preferred_element_type=F32)


def _dot_nt(a, b):
    return lax.dot_general(a, b, (((1,), (1,)), ((), ())), preferred_element_type=F32)


def _dot_tn(a, b):
    return lax.dot_general(a, b, (((0,), (0,)), ((), ())), preferred_element_type=F32)


def _silu(x):
    return x * jax.nn.sigmoid(x)


def _gelu(x):
    return 0.5 * x * (1.0 + lax.erf(x * 0.7071067811865476))


def _params(*sem):
    return pltpu.CompilerParams(dimension_semantics=sem, vmem_limit_bytes=VMEM_LIMIT)


def _head_slice(h):
    return slice(h * HEAD_D, (h + 1) * HEAD_D)


def _rows(start, size):
    return pl.ds(start if isinstance(start, int) else pl.multiple_of(start, size), size)


def _inproj_body(tiles_per_seq, x_ref, xp_ref, xn_ref, g_ref, w_ref, wab_ref, lng_ref, lnb_ref, ws_ref, bias_ref,
                 cw_ref, o_ref, ab_ref, ya_ref, pc_s):
    tm = x_ref.shape[0]

    def normed(x):
        return (x * lax.rsqrt(jnp.mean(x * x, axis=-1, keepdims=True) + EPS) * g_ref[...]).astype(BF16)

    hb = normed(x_ref[...])
    hb_ext = jnp.concatenate([hb, normed(jnp.concatenate([xp_ref[...], xn_ref[...]], axis=0))], axis=0)
    ab_ref[...] = _dot(hb, wab_ref[...])
    gm = _dot(hb, w_ref[:, 0:GM_COLS])
    o_ref[:, 0:PLAIN_COLS] = _dot(hb, w_ref[:, GM_COLS:GM_COLS + PLAIN_COLS]).astype(o_ref.dtype)
    pc = _dot(hb_ext, w_ref[:, GM_COLS + PLAIN_COLS:MAIN_COLS])

    t_in_seq = pl.program_id(0) % tiles_per_seq
    keep_before = jnp.where(t_in_seq != 0, 1.0, 0.0)
    keep_after = jnp.where(t_in_seq != tiles_per_seq - 1, 1.0, 0.0)
    for s in range(CONV_COLS // LANES):
        cs = slice(s * LANES, (s + 1) * LANES)
        pc_s[s, 0:HALO, :] = pc[tm:tm + HALO, cs] * keep_before
        pc_s[s, HALO:HALO + tm, :] = pc[0:tm, cs]
        pc_s[s, HALO + tm:2 * HALO + tm, :] = pc[tm + HALO:tm + 2 * HALO, cs] * keep_after
    for s in range(CONV_COLS // LANES):
        cs = slice(s * LANES, (s + 1) * LANES)
        w = cw_ref[:, cs]
        piece = s // HEADS
        for j in range(tm // CONV_TILE):
            base = HALO + j * CONV_TILE
            y = _silu(pc_s[s, base - 1:base - 1 + CONV_TILE, :] * w[0:1, :]
                      + pc_s[s, base:base + CONV_TILE, :] * w[1:2, :]
                      + pc_s[s, base + 1:base + 1 + CONV_TILE, :] * w[2:3, :])
            if piece < 2:
                scale = HEAD_D ** -0.5 if piece == 0 else 1.0
                y = y * (lax.rsqrt(jnp.sum(y * y, axis=-1, keepdims=True) + EPS) * scale)
            o_ref[j * CONV_TILE:(j + 1) * CONV_TILE, PLAIN_COLS + s * LANES:PLAIN_COLS + (s + 1) * LANES] = (
                y.astype(o_ref.dtype))

    for c in range(tm // GM_CHUNK):
        r = slice(c * GM_CHUNK, (c + 1) * GM_CHUNK)
        u = _gelu(gm[r, 0:BRANCH_W])
        v = _gelu(gm[r, BRANCH_W:2 * BRANCH_W])
        z = gm[r, 2 * BRANCH_W:3 * BRANCH_W]
        mu = jnp.mean(v, axis=-1, keepdims=True)
        vc = v - mu
        var = jnp.mean(vc * vc, axis=-1, keepdims=True)
        vn = (vc * lax.rsqrt(var + EPS) * lng_ref[...] + lnb_ref[...]).astype(BF16)
        gate = u * _silu(z)
        for g in range(GM_GROUPS):
            cs = slice(g * LANES, (g + 1) * LANES)
            mixed = _dot(ws_ref[g], vn[:, cs]) + bias_ref[:, cs]
            ya_ref[r, cs] = (gate[:, cs] * mixed).astype(ya_ref.dtype)


def _in_proj(x, g, w_main, w_ab, ln_g, ln_b, w_s, bias, conv_w, seq, tm=512):
    tokens = x.shape[0]
    halo_blocks = tm // HALO
    row = lambda w: pl.BlockSpec((tm, w), lambda i: (i, 0))
    full = lambda shape, **kw: pl.BlockSpec(shape, lambda i: (0,) * len(shape), **kw)
    before = pl.BlockSpec((HALO, D_MODEL), lambda i: (jnp.maximum(i * halo_blocks - 1, 0), 0))
    after = pl.BlockSpec((HALO, D_MODEL), lambda i: (jnp.minimum((i + 1) * halo_blocks, tokens // HALO - 1), 0))
    return pl.pallas_call(
        functools.partial(_inproj_body, seq // tm),
        grid=(tokens // tm,),
        in_specs=[row(D_MODEL), before, after, full((1, D_MODEL)),
                  full((D_MODEL, MAIN_COLS), pipeline_mode=pl.Buffered(1)), full((D_MODEL, LANES)),
                  full((1, BRANCH_W)), full((1, BRANCH_W)), full((GM_GROUPS, GM_CHUNK, GM_CHUNK)),
                  full((GM_CHUNK, BRANCH_W)), full((3, CONV_COLS))],
        out_specs=[row(MIX_COLS), row(LANES), row(BRANCH_W)],
        out_shape=[
            jax.ShapeDtypeStruct((tokens, MIX_COLS), BF16),
            jax.ShapeDtypeStruct((tokens, LANES), F32),
            jax.ShapeDtypeStruct((tokens, BRANCH_W), BF16),
        ],
        scratch_shapes=[pltpu.VMEM((CONV_COLS // LANES, tm + 2 * HALO, LANES), F32)],
        compiler_params=_params("arbitrary"),
        name="in_proj",
    )(x, x, x, g, w_main, w_ab, ln_g, ln_b, w_s, bias, conv_w)


def _ret_body(q_ref, k_ref, v_ref, z_ref, cos_ref, sin_ref, dl_ref, ng_ref, y_ref, qr_s, kr_s, kv_s):
    C = RET_CHUNK
    n_chunks = q_ref.shape[0] // C
    row = lax.broadcasted_iota(jnp.int32, (C, C), 0).astype(F32)
    colm = lax.broadcasted_iota(jnp.int32, (C, C), 1).astype(F32)
    rel = row - colm

    decs = []
    for h in range(HEADS):
        lf = jnp.broadcast_to(jax.nn.log_sigmoid(dl_ref[0, h])[0:1, :], (C, C))
        lb = jnp.broadcast_to(jax.nn.log_sigmoid(dl_ref[1, h])[0:1, :], (C, C))
        decs.append(dict(
            mask=jnp.exp(jnp.where(rel >= 0, lf * rel, -lb * rel)),
            qdf=jnp.exp(lf * (row + 1.0)), kdf=jnp.exp(lf * (C - 1.0 - row)),
            qdb=jnp.exp(lb * (C - row)), kdb=jnp.exp(lb * row),
            cf=jnp.exp(lf * C), cb=jnp.exp(lb * C)))

    def phase1(it, carry):
        cs = [it * RET_UNROLL + t for t in range(RET_UNROLL)]
        rows = [_rows(c * C, C) for c in cs]
        items = [(t, h) for t in range(RET_UNROLL) for h in range(HEADS)]
        cos = [cos_ref[r, :] for r in rows]
        sin = [sin_ref[r, :] for r in rows]
        qs = [q_ref[rows[t], _head_slice(h)].astype(F32) for t, h in items]
        ks = [k_ref[rows[t], _head_slice(h)].astype(F32) for t, h in items]
        qr = [q * cos[t] + pltpu.roll(q, HEAD_D // 2, 1) * sin[t] for (t, h), q in zip(items, qs)]
        kr = [(k * cos[t] + pltpu.roll(k, HEAD_D // 2, 1) * sin[t]) * (HEAD_D ** -0.5) for (t, h), k in zip(items, ks)]
        kcat = [jnp.concatenate([k * decs[h]["kdf"], k * decs[h]["kdb"]], axis=1).astype(BF16)
                for (t, h), k in zip(items, kr)]
        kvs = [_dot_tn(kc, v_ref[rows[t], _head_slice(h)]) for (t, h), kc in zip(items, kcat)]
        for n, (t, h) in enumerate(items):
            qr_s[rows[t], _head_slice(h)] = qr[n].astype(qr_s.dtype)
            kr_s[rows[t], _head_slice(h)] = kr[n].astype(kr_s.dtype)
            kv_s[h, cs[t]] = kvs[n]
        return carry

    lax.fori_loop(0, n_chunks // RET_UNROLL, phase1, 0)

    for h in range(HEADS):
        cf, cb = decs[h]["cf"], decs[h]["cb"]

        def scan(i, carry, h=h, cf=cf, cb=cb):
            sf, sb = carry
            j = n_chunks - 1 - i
            upd_f = kv_s[h, i, 0:HEAD_D, :]
            kv_s[h, i, 0:HEAD_D, :] = sf
            upd_b = kv_s[h, j, HEAD_D:2 * HEAD_D, :]
            kv_s[h, j, HEAD_D:2 * HEAD_D, :] = sb
            return sf * cf + upd_f, sb * cb + upd_b

        zero = jnp.zeros((HEAD_D, HEAD_D), F32)
        lax.fori_loop(0, n_chunks, scan, (zero, zero))

    def phase3(it, carry):
        cs = [it * RET_UNROLL + t for t in range(RET_UNROLL)]
        rows = [_rows(c * C, C) for c in cs]
        items = [(t, h) for t in range(RET_UNROLL) for h in range(HEADS)]
        qr = [qr_s[rows[t], _head_slice(h)] for t, h in items]
        sc = [_dot_nt(q, kr_s[rows[t], _head_slice(h)]) * decs[h]["mask"] for (t, h), q in zip(items, qr)]
        lhs, rhs = [], []
        for n, (t, h) in enumerate(items):
            q = qr[n].astype(F32)
            lhs.append(jnp.concatenate([sc[n], q * decs[h]["qdf"], q * decs[h]["qdb"]], axis=1).astype(BF16))
            rhs.append(jnp.concatenate([v_ref[rows[t], _head_slice(h)], kv_s[h, cs[t]].astype(BF16)], axis=0))
        outs = [_dot(a, b) for a, b in zip(lhs, rhs)]
        mus = [jnp.mean(o, axis=-1, keepdims=True) for o in outs]
        ocs = [o - mu for o, mu in zip(outs, mus)]
        var = [jnp.mean(oc * oc, axis=-1, keepdims=True) for oc in ocs]
        for n, (t, h) in enumerate(items):
            hs = _head_slice(h)
            on = ocs[n] * lax.rsqrt(var[n] + EPS) * ng_ref[:, hs]
            y_ref[rows[t], hs] = (on * _silu(z_ref[rows[t], hs].astype(F32))).astype(y_ref.dtype)
        return carry

    lax.fori_loop(0, n_chunks // RET_UNROLL, phase3, 0)


def _retention(proj, cos, sin, dl, ng):
    seq = cos.shape[0]
    batch = proj.shape[0] // seq
    col = lambda c: pl.BlockSpec((seq, BRANCH_W), lambda b, c=c: (b, c))
    full = lambda shape: pl.BlockSpec(shape, lambda b: (0,) * len(shape))
    return pl.pallas_call(
        _ret_body,
        grid=(batch,),
        in_specs=[col(COL_RQ), col(COL_RK), col(COL_RV), col(COL_RZ),
                  full((seq, HEAD_D)), full((seq, HEAD_D)), full((2, HEADS, 8, LANES)), full((1, BRANCH_W))],
        out_specs=pl.BlockSpec((seq, BRANCH_W), lambda b: (b, 0)),
        out_shape=jax.ShapeDtypeStruct((batch * seq, BRANCH_W), BF16),
        scratch_shapes=[pltpu.VMEM((seq, BRANCH_W), BF16), pltpu.VMEM((seq, BRANCH_W), BF16),
                        pltpu.VMEM((HEADS, seq // RET_CHUNK, 2 * HEAD_D, HEAD_D), F32)],
        compiler_params=_params("arbitrary"),
        name="retention",
    )(proj, proj, proj, proj, cos, sin, dl, ng)


def _gdn_body(qn_s, kn_s, vn_s, z_ref, ab_ref, alog_ref, dtb_ref, ng_ref, y_ref,
              g_s, gcf_s, gcb_s, u_s, w_s, sc_s, of_s, ob_s, st_s):
    C = GDN_CHUNK
    seq = qn_s.shape[0]
    n_chunks = seq // C

    srow = lax.broadcasted_iota(jnp.int32, (seq, LANES), 0)
    ab = ab_ref[...]
    lane = lax.broadcasted_iota(jnp.int32, (seq, LANES), 1)
    g_all = jnp.where((lane & 7) < HEADS,
                      -jnp.exp(alog_ref[...]) * jax.nn.softplus(ab + dtb_ref[...]),
                      jax.nn.sigmoid(ab))
    g_s[...] = g_all
    pos = srow & (C - 1)
    gf = g_all
    gb = g_all
    s = 1
    while s < C:
        gf = gf + jnp.where(pos >= s, pltpu.roll(gf, s, 0), 0.0)
        gb = gb + jnp.where(pos < C - s, pltpu.roll(gb, seq - s, 0), 0.0)
        s *= 2
    gcf_s[...] = gf
    gcb_s[...] = gb

    ri = lax.broadcasted_iota(jnp.int32, (C, C), 0)
    ci = lax.broadcasted_iota(jnp.int32, (C, C), 1)
    eye = (ri == ci).astype(F32)
    masks = ((ri >= ci, ri > ci), (ri <= ci, ri < ci))
    levels = [((ri >> (j + 1)) == (ci >> (j + 1))) & ((ri >> j) != (ci >> j)) for j in range(6)]
    chains = [(d, h) for d in range(2) for h in range(HEADS)]
    gc_refs = (gcf_s, gcb_s)

    U = PHASE_A_CHUNKS
    n_groups = n_chunks // U

    def group_rows(grp):
        return ([_rows((grp * U + t) * C, C) for t in range(U)],
                [_rows((n_chunks - 1 - grp * U - t) * C, C) for t in range(U)])

    def part_a(grp):
        rows = group_rows(grp)
        work = [(t, d, h) for t in range(U) for d, h in chains]
        gcs = {(t, d): gc_refs[d][rows[d][t], :] for t in range(U) for d in range(2)}
        gts = {key: g.T for key, g in gcs.items()}
        gbeta = {(t, d): g_s[rows[d][t], :] for t in range(U) for d in range(2)}
        tri, rhs = [], []
        for t, d, h in work:
            r, hs = rows[d][t], _head_slice(h)
            incl, strict = masks[d]
            gl, bl = 8 * d + h, 8 * d + HEADS + h
            k16 = kn_s[r, hs]
            kk = _dot_nt(jnp.concatenate([k16, qn_s[r, hs]], axis=0), k16)
            gcol = jnp.broadcast_to(gcs[t, d][:, gl:gl + 1], (C, HEAD_D))
            beta = jnp.broadcast_to(gbeta[t, d][:, bl:bl + 1], (C, HEAD_D))
            grow = jnp.broadcast_to(gts[t, d][gl:gl + 1, :], (C, C))
            decay = jnp.exp(jnp.where(incl, gcol[:, :C] - grow, -jnp.inf))
            tri.append(jnp.where(strict, beta[:, :C] * kk[:C] * decay, 0.0))
            sc_s[d, h, r, :] = (kk[C:] * decay).astype(sc_s.dtype)
            kbeta = k16.astype(F32) * beta
            rhs.append(jnp.concatenate([vn_s[r, hs].astype(F32) * beta, kbeta * jnp.exp(gcol)],
                                       axis=1).astype(BF16))
        inv = [eye - jnp.where(levels[0], x, 0.0) for x in tri]
        for j in range(1, 6):
            m16 = [jnp.where(levels[j], x, 0.0).astype(BF16) for x in tri]
            i16 = [p.astype(BF16) for p in inv]
            mi = [_dot(m, p).astype(BF16) for m, p in zip(m16, i16)]
            inv = [p - _dot(p16, x) for p, p16, x in zip(inv, i16, mi)]
        for (t, d, h), p, b in zip(work, inv, rhs):
            sol = _dot(p.astype(BF16), b)
            u_s[d, rows[d][t], _head_slice(h)] = sol[:, :HEAD_D].astype(u_s.dtype)
            w_s[d, rows[d][t], _head_slice(h)] = sol[:, HEAD_D:].astype(w_s.dtype)

    o_refs = (of_s, ob_s)

    def part_b(grp):
        rows = group_rows(grp)
        st = [st_s[n] for n in range(len(chains))]
        for t in range(U):
            gcs = [gc_refs[d][rows[d][t], :] for d in range(2)]
            lhs, kd, egl = [], [], []
            for d, h in chains:
                r, hs = rows[d][t], _head_slice(h)
                gl = 8 * d + h
                gcol = jnp.broadcast_to(gcs[d][:, gl:gl + 1], (C, HEAD_D))
                last = C - 1 if d == 0 else 0
                glast = gcol[last:last + 1, :]
                qe = (qn_s[r, hs].astype(F32) * jnp.exp(gcol)).astype(BF16)
                lhs.append(jnp.concatenate([w_s[d, r, hs], qe], axis=0))
                kd.append((kn_s[r, hs].astype(F32) * jnp.exp(glast - gcol)).astype(BF16))
                egl.append(jnp.exp(glast))
            ws = [_dot(a, s_.astype(BF16)) for a, s_ in zip(lhs, st)]
            vnew = [(u_s[d, rows[d][t], _head_slice(h)].astype(F32) - x[:C]).astype(BF16)
                    for (d, h), x in zip(chains, ws)]
            for n, (d, h) in enumerate(chains):
                o_refs[d][rows[d][t], _head_slice(h)] = (
                    ws[n][C:] + _dot(sc_s[d, h, rows[d][t], :], vnew[n])).astype(of_s.dtype)
            st = [s_ * e + _dot_tn(a, x) for s_, e, a, x in zip(st, egl, kd, vnew)]
        for n, s_ in enumerate(st):
            st_s[n] = s_

    st_s[...] = jnp.zeros_like(st_s)
    part_a(0)

    def fused(grp, carry):
        part_b(grp - 1)
        part_a(grp)
        return carry

    lax.fori_loop(1, n_groups, fused, 0)
    part_b(n_groups - 1)

    def finish(i, carry):
        r = _rows(i * 2 * C, 2 * C)
        for h in range(HEADS):
            hs = _head_slice(h)
            o = of_s[r, hs].astype(F32) + ob_s[r, hs].astype(F32)
            on = o * lax.rsqrt(jnp.mean(o * o, axis=-1, keepdims=True) + EPS) * ng_ref[...]
            y_ref[r, hs] = (on * _silu(z_ref[r, hs].astype(F32))).astype(y_ref.dtype)
        return carry

    lax.fori_loop(0, n_chunks // 2, finish, 0)


def _gdn(proj, ab, alog_row, dtb_row, ng, seq):
    batch = proj.shape[0] // seq
    col = lambda c: pl.BlockSpec((seq, BRANCH_W), lambda b, c=c: (b, c), pipeline_mode=pl.Buffered(1))
    full = lambda shape: pl.BlockSpec(shape, lambda b: (0,) * len(shape))
    big = lambda dt: pltpu.VMEM((seq, BRANCH_W), dt)
    small = pltpu.VMEM((seq, LANES), F32)
    return pl.pallas_call(
        _gdn_body,
        grid=(batch,),
        in_specs=[col(COL_DQ), col(COL_DK), col(COL_DV), col(COL_DZ),
                  pl.BlockSpec((seq, LANES), lambda b: (b, 0)),
                  full((1, LANES)), full((1, LANES)), full((1, HEAD_D))],
        out_specs=pl.BlockSpec((seq, BRANCH_W), lambda b: (b, 0)),
        out_shape=jax.ShapeDtypeStruct((batch * seq, BRANCH_W), BF16),
        scratch_shapes=[small, small, small,
                        pltpu.VMEM((2, seq, BRANCH_W), BF16), pltpu.VMEM((2, seq, BRANCH_W), BF16),
                        pltpu.VMEM((2, HEADS, seq, GDN_CHUNK), BF16),
                        big(BF16), big(BF16), pltpu.VMEM((2 * HEADS, HEAD_D, HEAD_D), F32)],
        compiler_params=_params("arbitrary"),
        name="gdn",
    )(proj, proj, proj, proj, ab, alog_row, dtb_row, ng)


def _merge_body(final, x_ref, g_ref, ya_ref, yb_ref, yc_ref, wg_ref, wbo_ref, wo_ref, *rest):
    if final:
        fg_ref, o_ref = rest
    else:
        (o_ref,) = rest
    x = x_ref[...]
    h = (x * lax.rsqrt(jnp.mean(x * x, axis=-1, keepdims=True) + EPS) * g_ref[...]).astype(BF16)
    merged = None
    for n, y_ref in enumerate((ya_ref, yb_ref, yc_ref)):
        gate = jax.nn.sigmoid(_dot(h, wg_ref[n]))
        term = gate * _dot(y_ref[...], wbo_ref[n])
        merged = term if merged is None else merged + term
    out = x + _dot(merged.astype(BF16), wo_ref[...])
    if final:
        out = out * lax.rsqrt(jnp.mean(out * out, axis=-1, keepdims=True) + EPS) * fg_ref[...]
    o_ref[...] = out


def _merge(x, g, ya, yb, yc, wg, wbo, wo, final_g=None, tm=512):
    tokens = x.shape[0]
    final = final_g is not None
    row = lambda w: pl.BlockSpec((tm, w), lambda i: (i, 0))
    full = lambda shape: pl.BlockSpec(shape, lambda i: (0,) * len(shape))
    in_specs = [row(D_MODEL), full((1, D_MODEL)), row(BRANCH_W), row(BRANCH_W), row(BRANCH_W),
                full((N_BRANCH, D_MODEL, D_MODEL)), full((N_BRANCH, BRANCH_W, D_MODEL)), full((D_MODEL, D_MODEL))]
    args = [x, g, ya, yb, yc, wg, wbo, wo]
    if final:
        in_specs.append(full((1, D_MODEL)))
        args.append(final_g)
    return pl.pallas_call(
        functools.partial(_merge_body, final),
        grid=(tokens // tm,),
        in_specs=in_specs,
        out_specs=row(D_MODEL),
        out_shape=jax.ShapeDtypeStruct((tokens, D_MODEL), F32),
        compiler_params=_params("arbitrary"),
        name="merge_final" if final else "merge",
    )(*args)


def _rope_tables(seq):
    inv = ROPE_BASE ** (-jnp.arange(0, HEAD_D, 2, dtype=F32) / HEAD_D)
    ang = jnp.arange(seq, dtype=F32)[:, None] * inv[None, :]
    cos = jnp.cos(ang)
    sin = jnp.sin(ang)
    return jnp.concatenate([cos, cos], axis=-1), jnp.concatenate([-sin, sin], axis=-1)


def _lane_row(p):
    row = jnp.zeros((LANES,), F32)
    row = row.at[0:HEADS].set(p[0]).at[2 * HEADS:3 * HEADS].set(p[1])
    return row[None, :]


def kernel(x, norm_g, w_in, gm_ln_g, gm_ln_b, gm_w_s, gm_b_s, ret_decay_logit, ret_norm_g, gdn_conv_w, gdn_a_log, gdn_dt_bias, gdn_norm_g, w_gate, w_branch_out, w_out, final_norm_g):
    batch, seq, _ = x.shape
    xf = x.reshape(batch * seq, D_MODEL)
    cos, sin = _rope_tables(seq)
    for l in range(DEPTH):
        w_l = w_in[l]
        w_main = jnp.concatenate([w_l[:, :GM_COLS + 4 * BRANCH_W], w_l[:, MAIN_COLS - BRANCH_W:MAIN_COLS],
                                  w_l[:, GM_COLS + 4 * BRANCH_W:MAIN_COLS - BRANCH_W]], axis=1).astype(BF16)
        w_ab = jnp.pad(w_in[l][:, MAIN_COLS:], ((0, 0), (0, LANES - 4 * HEADS))).astype(BF16)
        g_row = norm_g[l][None, :]
        bias = jnp.repeat(gm_b_s[l].T, GM_CHUNK, axis=1)
        proj, ab, ya = _in_proj(xf, g_row, w_main, w_ab, gm_ln_g[l][None, :], gm_ln_b[l][None, :],
                                gm_w_s[l].astype(BF16), bias, gdn_conv_w[l], seq)

        dl = jnp.broadcast_to(ret_decay_logit[l][:, :, None, None], (2, HEADS, 8, LANES))
        yb = _retention(proj, cos, sin, dl, ret_norm_g[l][None, :])

        yc = _gdn(proj, ab, _lane_row(gdn_a_log[l]), _lane_row(gdn_dt_bias[l]), gdn_norm_g[l][None, :], seq)

        wg = jnp.transpose(w_gate[l], (1, 0, 2)).astype(BF16)
        xf = _merge(xf, g_row, ya, yb, yc, wg, w_branch_out[l].astype(BF16), w_out[l].astype(BF16),
                    final_norm_g[None, :] if l == DEPTH - 1 else None)
    return xf.reshape(batch, seq, D_MODEL)
```

```python
import functools

import jax
import jax.numpy as jnp
from jax import lax
from jax.experimental import pallas as pl
from jax.experimental.pallas import tpu as pltpu

F32 = jnp.float32
BF16 = jnp.bfloat16

D_MODEL = 1024
DEPTH = 2
BRANCH_W = 512
N_BRANCH = 3
GM_GROUPS = 4
GM_CHUNK = 128
HEADS = 4
HEAD_D = 128
RET_CHUNK = 128
RET_UNROLL = 4
GDN_CHUNK = 64
PHASE_A_CHUNKS = 4
CONV_TILE = 128
ROPE_BASE = 10000.0
EPS = 1e-6
LANES = 128
MAIN_COLS = 11 * BRANCH_W
GM_COLS = 3 * BRANCH_W
PLAIN_COLS = 5 * BRANCH_W
CONV_COLS = 3 * BRANCH_W
MIX_COLS = PLAIN_COLS + CONV_COLS
COL_RQ, COL_RK, COL_RV, COL_RZ = 0, 1, 2, 3
COL_DZ, COL_DQ, COL_DK, COL_DV = 4, 5, 6, 7
HALO = 8

VMEM_LIMIT = 56 * 1024 * 1024


def _dot(a, b):
    return jnp.dot(a, b, preferred_element_type=F32)


def _dot_nt(a, b):
    return lax.dot_general(a, b, (((1,), (1,)), ((), ())), preferred_element_type=F32)


def _dot_tn(a, b):
    return lax.dot_general(a, b, (((0,), (0,)), ((), ())), preferred_element_type=F32)


def _silu(x):
    return x * jax.nn.sigmoid(x)


def _gelu(x):
    return 0.5 * x * (1.0 + lax.erf(x * 0.7071067811865476))


def _params(*sem):
    return pltpu.CompilerParams(dimension_semantics=sem, vmem_limit_bytes=VMEM_LIMIT)


def _head_slice(h):
    return slice(h * HEAD_D, (h + 1) * HEAD_D)


def _rows(start, size):
    return pl.ds(start if isinstance(start, int) else pl.multiple_of(start, size), size)


def _inproj_body(tiles_per_seq, x_ref, xp_ref, xn_ref, g_ref, w_ref, wab_ref, lng_ref, lnb_ref, ws_ref, bias_ref,
                 cw_ref, alog_ref, dtb_ref, o_ref, gb_ref, gcf_ref, gcb_ref, ya_ref, pc_s):
    tm = x_ref.shape[0]

    def normed(x):
        return (x * lax.rsqrt(jnp.mean(x * x, axis=-1, keepdims=True) + EPS) * g_ref[...]).astype(BF16)

    hb = normed(x_ref[...])
    hb_ext = jnp.concatenate([hb, normed(jnp.concatenate([xp_ref[...], xn_ref[...]], axis=0))], axis=0)
    ab = _dot(hb, wab_ref[...])
    gm = _dot(hb, w_ref[:, 0:GM_COLS])
    o_ref[:, 0:PLAIN_COLS] = _dot(hb, w_ref[:, GM_COLS:GM_COLS + PLAIN_COLS]).astype(o_ref.dtype)
    pc = _dot(hb_ext, w_ref[:, GM_COLS + PLAIN_COLS:MAIN_COLS])

    t_in_seq = pl.program_id(0) % tiles_per_seq
    keep_before = jnp.where(t_in_seq != 0, 1.0, 0.0)
    keep_after = jnp.where(t_in_seq != tiles_per_seq - 1, 1.0, 0.0)
    for s in range(CONV_COLS // LANES):
        cs = slice(s * LANES, (s + 1) * LANES)
        pc_s[s, 0:HALO, :] = pc[tm:tm + HALO, cs] * keep_before
        pc_s[s, HALO:HALO + tm, :] = pc[0:tm, cs]
        pc_s[s, HALO + tm:2 * HALO + tm, :] = pc[tm + HALO:tm + 2 * HALO, cs] * keep_after
    for s in range(CONV_COLS // LANES):
        cs = slice(s * LANES, (s + 1) * LANES)
        w = cw_ref[:, cs]
        piece = s // HEADS
        for j in range(tm // CONV_TILE):
            base = HALO + j * CONV_TILE
            y = _silu(pc_s[s, base - 1:base - 1 + CONV_TILE, :] * w[0:1, :]
                      + pc_s[s, base:base + CONV_TILE, :] * w[1:2, :]
                      + pc_s[s, base + 1:base + 1 + CONV_TILE, :] * w[2:3, :])
            if piece < 2:
                scale = HEAD_D ** -0.5 if piece == 0 else 1.0
                y = y * (lax.rsqrt(jnp.sum(y * y, axis=-1, keepdims=True) + EPS) * scale)
            o_ref[j * CONV_TILE:(j + 1) * CONV_TILE, PLAIN_COLS + s * LANES:PLAIN_COLS + (s + 1) * LANES] = (
                y.astype(o_ref.dtype))

    lane = lax.broadcasted_iota(jnp.int32, (tm, LANES), 1)
    pos = lax.broadcasted_iota(jnp.int32, (tm, LANES), 0) & (GDN_CHUNK - 1)
    g_all = jnp.where((lane & 7) < HEADS,
                      -jnp.exp(alog_ref[...]) * jax.nn.softplus(ab + dtb_ref[...]),
                      jax.nn.sigmoid(ab))
    gb_ref[...] = g_all
    gf = g_all
    gb = g_all
    s = 1
    while s < GDN_CHUNK:
        gf = gf + jnp.where(pos >= s, pltpu.roll(gf, s, 0), 0.0)
        gb = gb + jnp.where(pos < GDN_CHUNK - s, pltpu.roll(gb, tm - s, 0), 0.0)
        s *= 2
    gcf_ref[...] = gf
    gcb_ref[...] = gb

    for c in range(tm // GM_CHUNK):
        r = slice(c * GM_CHUNK, (c + 1) * GM_CHUNK)
        u = _gelu(gm[r, 0:BRANCH_W])
        v = _gelu(gm[r, BRANCH_W:2 * BRANCH_W])
        z = gm[r, 2 * BRANCH_W:3 * BRANCH_W]
        mu = jnp.mean(v, axis=-1, keepdims=True)
        vc = v - mu
        var = jnp.mean(vc * vc, axis=-1, keepdims=True)
        vn = (vc * lax.rsqrt(var + EPS) * lng_ref[...] + lnb_ref[...]).astype(BF16)
        gate = u * _silu(z)
        for g in range(GM_GROUPS):
            cs = slice(g * LANES, (g + 1) * LANES)
            mixed = _dot(ws_ref[g], vn[:, cs]) + bias_ref[:, cs]
            ya_ref[r, cs] = (gate[:, cs] * mixed).astype(ya_ref.dtype)


def _in_proj(x, g, w_main, w_ab, ln_g, ln_b, w_s, bias, conv_w, alog_row, dtb_row, seq, tm=512):
    tokens = x.shape[0]
    halo_blocks = tm // HALO
    row = lambda w: pl.BlockSpec((tm, w), lambda i: (i, 0))
    full = lambda shape, **kw: pl.BlockSpec(shape, lambda i: (0,) * len(shape), **kw)
    before = pl.BlockSpec((HALO, D_MODEL), lambda i: (jnp.maximum(i * halo_blocks - 1, 0), 0))
    after = pl.BlockSpec((HALO, D_MODEL), lambda i: (jnp.minimum((i + 1) * halo_blocks, tokens // HALO - 1), 0))
    return pl.pallas_call(
        functools.partial(_inproj_body, seq // tm),
        grid=(tokens // tm,),
        in_specs=[row(D_MODEL), before, after, full((1, D_MODEL)),
                  full((D_MODEL, MAIN_COLS), pipeline_mode=pl.Buffered(1)), full((D_MODEL, LANES)),
                  full((1, BRANCH_W)), full((1, BRANCH_W)), full((GM_GROUPS, GM_CHUNK, GM_CHUNK)),
                  full((GM_CHUNK, BRANCH_W)), full((3, CONV_COLS)), full((1, LANES)), full((1, LANES))],
        out_specs=[row(MIX_COLS), row(LANES), row(LANES), row(LANES), row(BRANCH_W)],
        out_shape=[
            jax.ShapeDtypeStruct((tokens, MIX_COLS), BF16),
            jax.ShapeDtypeStruct((tokens, LANES), F32),
            jax.ShapeDtypeStruct((tokens, LANES), F32),
            jax.ShapeDtypeStruct((tokens, LANES), F32),
            jax.ShapeDtypeStruct((tokens, BRANCH_W), BF16),
        ],
        scratch_shapes=[pltpu.VMEM((CONV_COLS // LANES, tm + 2 * HALO, LANES), F32)],
        compiler_params=_params("arbitrary"),
        name="in_proj",
    )(x, x, x, g, w_main, w_ab, ln_g, ln_b, w_s, bias, conv_w, alog_row, dtb_row)


def _ret_body(q_ref, k_ref, v_ref, z_ref, cos_ref, sin_ref, dl_ref, ng_ref, y_ref, qr_s, kr_s, kv_s):
    C = RET_CHUNK
    n_chunks = q_ref.shape[0] // C
    row = lax.broadcasted_iota(jnp.int32, (C, C), 0).astype(F32)
    colm = lax.broadcasted_iota(jnp.int32, (C, C), 1).astype(F32)
    rel = row - colm

    decs = []
    for h in range(HEADS):
        lf = jnp.broadcast_to(jax.nn.log_sigmoid(dl_ref[0, h])[0:1, :], (C, C))
        lb = jnp.broadcast_to(jax.nn.log_sigmoid(dl_ref[1, h])[0:1, :], (C, C))
        decs.append(dict(
            mask=jnp.exp(jnp.where(rel >= 0, lf * rel, -lb * rel)),
            qdf=jnp.exp(lf * (row + 1.0)), kdf=jnp.exp(lf * (C - 1.0 - row)),
            qdb=jnp.exp(lb * (C - row)), kdb=jnp.exp(lb * row),
            cf=jnp.exp(lf * C), cb=jnp.exp(lb * C)))

    def phase1(it, carry):
        cs = [it * RET_UNROLL + t for t in range(RET_UNROLL)]
        rows = [_rows(c * C, C) for c in cs]
        items = [(t, h) for t in range(RET_UNROLL) for h in range(HEADS)]
        cos = [cos_ref[r, :] for r in rows]
        sin = [sin_ref[r, :] for r in rows]
        qs = [q_ref[rows[t], _head_slice(h)].astype(F32) for t, h in items]
        ks = [k_ref[rows[t], _head_slice(h)].astype(F32) for t, h in items]
        qr = [q * cos[t] + pltpu.roll(q, HEAD_D // 2, 1) * sin[t] for (t, h), q in zip(items, qs)]
        kr = [(k * cos[t] + pltpu.roll(k, HEAD_D // 2, 1) * sin[t]) * (HEAD_D ** -0.5) for (t, h), k in zip(items, ks)]
        kcat = [jnp.concatenate([k * decs[h]["kdf"], k * decs[h]["kdb"]], axis=1).astype(BF16)
                for (t, h), k in zip(items, kr)]
        kvs = [_dot_tn(kc, v_ref[rows[t], _head_slice(h)]) for (t, h), kc in zip(items, kcat)]
        for n, (t, h) in enumerate(items):
            qr_s[rows[t], _head_slice(h)] = qr[n].astype(qr_s.dtype)
            kr_s[rows[t], _head_slice(h)] = kr[n].astype(kr_s.dtype)
            kv_s[h, cs[t]] = kvs[n]
        return carry

    lax.fori_loop(0, n_chunks // RET_UNROLL, phase1, 0)

    for h in range(HEADS):
        cf, cb = decs[h]["cf"], decs[h]["cb"]

        def scan(i, carry, h=h, cf=cf, cb=cb):
            sf, sb = carry
            j = n_chunks - 1 - i
            upd_f = kv_s[h, i, 0:HEAD_D, :]
            kv_s[h, i, 0:HEAD_D, :] = sf
            upd_b = kv_s[h, j, HEAD_D:2 * HEAD_D, :]
            kv_s[h, j, HEAD_D:2 * HEAD_D, :] = sb
            return sf * cf + upd_f, sb * cb + upd_b

        zero = jnp.zeros((HEAD_D, HEAD_D), F32)
        lax.fori_loop(0, n_chunks, scan, (zero, zero))

    def phase3(it, carry):
        cs = [it * RET_UNROLL + t for t in range(RET_UNROLL)]
        rows = [_rows(c * C, C) for c in cs]
        items = [(t, h) for t in range(RET_UNROLL) for h in range(HEADS)]
        qr = [qr_s[rows[t], _head_slice(h)] for t, h in items]
        sc = [_dot_nt(q, kr_s[rows[t], _head_slice(h)]) * decs[h]["mask"] for (t, h), q in zip(items, qr)]
        lhs, rhs = [], []
        for n, (t, h) in enumerate(items):
            q = qr[n].astype(F32)
            lhs.append(jnp.concatenate([sc[n], q * decs[h]["qdf"], q * decs[h]["qdb"]], axis=1).astype(BF16))
            rhs.append(jnp.concatenate([v_ref[rows[t], _head_slice(h)], kv_s[h, cs[t]].astype(BF16)], axis=0))
        outs = [_dot(a, b) for a, b in zip(lhs, rhs)]
        mus = [jnp.mean(o, axis=-1, keepdims=True) for o in outs]
        ocs = [o - mu for o, mu in zip(outs, mus)]
        var = [jnp.mean(oc * oc, axis=-1, keepdims=True) for oc in ocs]
        for n, (t, h) in enumerate(items):
            hs = _head_slice(h)
            on = ocs[n] * lax.rsqrt(var[n] + EPS) * ng_ref[:, hs]
            y_ref[rows[t], hs] = (on * _silu(z_ref[rows[t], hs].astype(F32))).astype(y_ref.dtype)
        return carry

    lax.fori_loop(0, n_chunks // RET_UNROLL, phase3, 0)


def _retention(proj, cos, sin, dl, ng):
    seq = cos.shape[0]
    batch = proj.shape[0] // seq
    col = lambda c: pl.BlockSpec((seq, BRANCH_W), lambda b, c=c: (b, c))
    full = lambda shape: pl.BlockSpec(shape, lambda b: (0,) * len(shape))
    return pl.pallas_call(
        _ret_body,
        grid=(batch,),
        in_specs=[col(COL_RQ), col(COL_RK), col(COL_RV), col(COL_RZ),
                  full((seq, HEAD_D)), full((seq, HEAD_D)), full((2, HEADS, 8, LANES)), full((1, BRANCH_W))],
        out_specs=pl.BlockSpec((seq, BRANCH_W), lambda b: (b, 0)),
        out_shape=jax.ShapeDtypeStruct((batch * seq, BRANCH_W), BF16),
        scratch_shapes=[pltpu.VMEM((seq, BRANCH_W), BF16), pltpu.VMEM((seq, BRANCH_W), BF16),
                        pltpu.VMEM((HEADS, seq // RET_CHUNK, 2 * HEAD_D, HEAD_D), F32)],
        compiler_params=_params("arbitrary"),
        name="retention",
    )(proj, proj, proj, proj, cos, sin, dl, ng)


def _gdn_body(qn_s, kn_s, vn_s, z_ref, g_s, gcf_s, gcb_s, ng_ref, y_ref, u_s, w_s, sc_s, tri_s, o_s, st_s):
    C = GDN_CHUNK
    seq = qn_s.shape[0]
    n_chunks = seq // C

    ri = lax.broadcasted_iota(jnp.int32, (C, C), 0)
    ci = lax.broadcasted_iota(jnp.int32, (C, C), 1)
    eye = (ri == ci).astype(F32)
    masks = ((ri >= ci, ri > ci), (ri <= ci, ri < ci))
    levels = [((ri >> (j + 1)) == (ci >> (j + 1))) & ((ri >> j) != (ci >> j)) for j in range(6)]
    chains = [(d, h) for d in range(2) for h in range(HEADS)]
    gc_refs = (gcf_s, gcb_s)

    U = PHASE_A_CHUNKS
    n_groups = n_chunks // U

    work = [(t, d, h) for t in range(U) for d, h in chains]

    def group_rows(grp):
        return ([_rows((grp * U + t) * C, C) for t in range(U)],
                [_rows((n_chunks - 1 - grp * U - t) * C, C) for t in range(U)])

    def prep(grp):
        rows = group_rows(grp)
        gcs = {(t, d): gc_refs[d][rows[d][t], :] for t in range(U) for d in range(2)}
        gts = {key: g.T for key, g in gcs.items()}
        gbeta = {(t, d): g_s[rows[d][t], :] for t in range(U) for d in range(2)}
        for t, d, h in work:
            r, hs = rows[d][t], _head_slice(h)
            incl, strict = masks[d]
            gl, bl = 8 * d + h, 8 * d + HEADS + h
            k16 = kn_s[r, hs]
            kk = _dot_nt(jnp.concatenate([k16, qn_s[r, hs]], axis=0), k16)
            gcol = gcs[t, d][:, gl:gl + 1]
            beta = gbeta[t, d][:, bl:bl + 1]
            grow = jnp.broadcast_to(gts[t, d][gl:gl + 1, :], (C, C))
            decay = jnp.exp(jnp.where(incl, gcol - grow, -jnp.inf))
            tri_s[d, h, r, :] = jnp.where(strict, beta * kk[:C] * decay, 0.0).astype(tri_s.dtype)
            sc_s[d, h, r, :] = (kk[C:] * decay).astype(sc_s.dtype)

    def solve(grp):
        rows = group_rows(grp)
        zero16 = jnp.zeros((C, C), BF16)
        tri = [tri_s[d, h, rows[d][t], :] for t, d, h in work]
        inv = [eye - jnp.where(levels[0], x, zero16).astype(F32) for x in tri]
        for j in range(1, 6):
            m16 = [jnp.where(levels[j], x, zero16) for x in tri]
            i16 = [p.astype(BF16) for p in inv]
            mi = [_dot(m, p).astype(BF16) for m, p in zip(m16, i16)]
            inv = [p - _dot(p16, x) for p, p16, x in zip(inv, i16, mi)]
        gcs = {(t, d): gc_refs[d][rows[d][t], :] for t in range(U) for d in range(2)}
        gbeta = {(t, d): g_s[rows[d][t], :] for t in range(U) for d in range(2)}
        for (t, d, h), p in zip(work, inv):
            r, hs = rows[d][t], _head_slice(h)
            gl, bl = 8 * d + h, 8 * d + HEADS + h
            gcol = jnp.broadcast_to(gcs[t, d][:, gl:gl + 1], (C, HEAD_D))
            beta = jnp.broadcast_to(gbeta[t, d][:, bl:bl + 1], (C, HEAD_D))
            kbeta = kn_s[r, hs].astype(F32) * beta
            rhs = jnp.concatenate([vn_s[r, hs].astype(F32) * beta, kbeta * jnp.exp(gcol)], axis=1).astype(BF16)
            sol = _dot(p.astype(BF16), rhs)
            u_s[d, r, hs] = sol[:, :HEAD_D].astype(u_s.dtype)
            w_s[d, r, hs] = sol[:, HEAD_D:].astype(w_s.dtype)

    def recur(grp, final):
        rows = group_rows(grp)
        st = [st_s[n] for n in range(len(chains))]
        for t in range(U):
            gcs = [gc_refs[d][rows[d][t], :] for d in range(2)]
            lhs, kd, egl = [], [], []
            for d, h in chains:
                r, hs = rows[d][t], _head_slice(h)
                gl = 8 * d + h
                gcol = jnp.broadcast_to(gcs[d][:, gl:gl + 1], (C, HEAD_D))
                last = C - 1 if d == 0 else 0
                glast = gcol[last:last + 1, :]
                qe = (qn_s[r, hs].astype(F32) * jnp.exp(gcol)).astype(BF16)
                lhs.append(jnp.concatenate([w_s[d, r, hs], qe], axis=0))
                kd.append((kn_s[r, hs].astype(F32) * jnp.exp(glast - gcol)).astype(BF16))
                egl.append(jnp.exp(glast))
            ws = [_dot(a, s_.astype(BF16)) for a, s_ in zip(lhs, st)]
            vnew = [(u_s[d, rows[d][t], _head_slice(h)].astype(F32) - x[:C]).astype(BF16)
                    for (d, h), x in zip(chains, ws)]
            outs = [ws[n][C:] + _dot(sc_s[d, h, rows[d][t], :], vnew[n]) for n, (d, h) in enumerate(chains)]
            for n, (d, h) in enumerate(chains):
                r, hs = rows[d][t], _head_slice(h)
                if final:
                    o = outs[n] + o_s[r, hs].astype(F32)
                    on = o * lax.rsqrt(jnp.mean(o * o, axis=-1, keepdims=True) + EPS) * ng_ref[...]
                    y_ref[r, hs] = (on * _silu(z_ref[r, hs].astype(F32))).astype(y_ref.dtype)
                else:
                    o_s[r, hs] = outs[n].astype(o_s.dtype)
            st = [s_ * e + _dot_tn(a, x) for s_, e, a, x in zip(st, egl, kd, vnew)]
        for n, s_ in enumerate(st):
            st_s[n] = s_

    half = n_groups // 2
    st_s[...] = jnp.zeros_like(st_s)
    prep(0)
    solve(0)
    prep(1)

    def pipelined(final, grp, carry):
        recur(grp - 2, final)
        solve(grp - 1)
        prep(grp)
        return carry

    lax.fori_loop(2, half + 2, functools.partial(pipelined, False), 0)
    lax.fori_loop(half + 2, n_groups, functools.partial(pipelined, True), 0)
    recur(n_groups - 2, True)
    solve(n_groups - 1)
    recur(n_groups - 1, True)


def _gdn(proj, g_all, gcf, gcb, ng, seq):
    batch = proj.shape[0] // seq
    col = lambda c: pl.BlockSpec((seq, BRANCH_W), lambda b, c=c: (b, c))
    lanes = pl.BlockSpec((seq, LANES), lambda b: (b, 0))
    full = lambda shape: pl.BlockSpec(shape, lambda b: (0,) * len(shape))
    per_chain = pltpu.VMEM((2, HEADS, seq, GDN_CHUNK), BF16)
    return pl.pallas_call(
        _gdn_body,
        grid=(batch,),
        in_specs=[col(COL_DQ), col(COL_DK), col(COL_DV), col(COL_DZ), lanes, lanes, lanes, full((1, HEAD_D))],
        out_specs=pl.BlockSpec((seq, BRANCH_W), lambda b: (b, 0)),
        out_shape=jax.ShapeDtypeStruct((batch * seq, BRANCH_W), BF16),
        scratch_shapes=[pltpu.VMEM((2, seq, BRANCH_W), BF16), pltpu.VMEM((2, seq, BRANCH_W), BF16),
                        per_chain, per_chain, pltpu.VMEM((seq, BRANCH_W), BF16),
                        pltpu.VMEM((2 * HEADS, HEAD_D, HEAD_D), F32)],
        compiler_params=_params("arbitrary"),
        name="gdn",
    )(proj, proj, proj, proj, g_all, gcf, gcb, ng)


def _merge_body(final, x_ref, g_ref, ya_ref, yb_ref, yc_ref, wg_ref, wbo_ref, wo_ref, *rest):
    if final:
        fg_ref, o_ref = rest
    else:
        (o_ref,) = rest
    x = x_ref[...]
    h = (x * lax.rsqrt(jnp.mean(x * x, axis=-1, keepdims=True) + EPS) * g_ref[...]).astype(BF16)
    merged = None
    for n, y_ref in enumerate((ya_ref, yb_ref, yc_ref)):
        gate = jax.nn.sigmoid(_dot(h, wg_ref[n]))
        term = gate * _dot(y_ref[...], wbo_ref[n])
        merged = term if merged is None else merged + term
    out = x + _dot(merged.astype(BF16), wo_ref[...])
    if final:
        out = out * lax.rsqrt(jnp.mean(out * out, axis=-1, keepdims=True) + EPS) * fg_ref[...]
    o_ref[...] = out


def _merge(x, g, ya, yb, yc, wg, wbo, wo, final_g=None, tm=512):
    tokens = x.shape[0]
    final = final_g is not None
    row = lambda w: pl.BlockSpec((tm, w), lambda i: (i, 0))
    full = lambda shape: pl.BlockSpec(shape, lambda i: (0,) * len(shape))
    in_specs = [row(D_MODEL), full((1, D_MODEL)), row(BRANCH_W), row(BRANCH_W), row(BRANCH_W),
                full((N_BRANCH, D_MODEL, D_MODEL)), full((N_BRANCH, BRANCH_W, D_MODEL)), full((D_MODEL, D_MODEL))]
    args = [x, g, ya, yb, yc, wg, wbo, wo]
    if final:
        in_specs.append(full((1, D_MODEL)))
        args.append(final_g)
    return pl.pallas_call(
        functools.partial(_merge_body, final),
        grid=(tokens // tm,),
        in_specs=in_specs,
        out_specs=row(D_MODEL),
        out_shape=jax.ShapeDtypeStruct((tokens, D_MODEL), F32),
        compiler_params=_params("arbitrary"),
        name="merge_final" if final else "merge",
    )(*args)


def _rope_tables(seq):
    inv = ROPE_BASE ** (-jnp.arange(0, HEAD_D, 2, dtype=F32) / HEAD_D)
    ang = jnp.arange(seq, dtype=F32)[:, None] * inv[None, :]
    cos = jnp.cos(ang)
    sin = jnp.sin(ang)
    return jnp.concatenate([cos, cos], axis=-1), jnp.concatenate([-sin, sin], axis=-1)


def _lane_row(p):
    row = jnp.zeros((LANES,), F32)
    row = row.at[0:HEADS].set(p[0]).at[2 * HEADS:3 * HEADS].set(p[1])
    return row[None, :]


def kernel(x, norm_g, w_in, gm_ln_g, gm_ln_b, gm_w_s, gm_b_s, ret_decay_logit, ret_norm_g, gdn_conv_w, gdn_a_log, gdn_dt_bias, gdn_norm_g, w_gate, w_branch_out, w_out, final_norm_g):
    batch, seq, _ = x.shape
    xf = x.reshape(batch * seq, D_MODEL)
    cos, sin = _rope_tables(seq)
    for l in range(DEPTH):
        w_l = w_in[l]
        w_main = jnp.concatenate([w_l[:, :GM_COLS + 4 * BRANCH_W], w_l[:, MAIN_COLS - BRANCH_W:MAIN_COLS],
                                  w_l[:, GM_COLS + 4 * BRANCH_W:MAIN_COLS - BRANCH_W]], axis=1).astype(BF16)
        w_ab = jnp.pad(w_in[l][:, MAIN_COLS:], ((0, 0), (0, LANES - 4 * HEADS))).astype(BF16)
        g_row = norm_g[l][None, :]
        bias = jnp.repeat(gm_b_s[l].T, GM_CHUNK, axis=1)
        proj, g_all, gcf, gcb, ya = _in_proj(
            xf, g_row, w_main, w_ab, gm_ln_g[l][None, :], gm_ln_b[l][None, :], gm_w_s[l].astype(BF16), bias,
            gdn_conv_w[l], _lane_row(gdn_a_log[l]), _lane_row(gdn_dt_bias[l]), seq)

        dl = jnp.broadcast_to(ret_decay_logit[l][:, :, None, None], (2, HEADS, 8, LANES))
        yb = _retention(proj, cos, sin, dl, ret_norm_g[l][None, :])

        yc = _gdn(proj, g_all, gcf, gcb, gdn_norm_g[l][None, :], seq)

        wg = jnp.transpose(w_gate[l], (1, 0, 2)).astype(BF16)
        xf = _merge(xf, g_row, ya, yb, yc, wg, w_branch_out[l].astype(BF16), w_out[l].astype(BF16),
                    final_norm_g[None, :] if l == DEPTH - 1 else None)
    return xf.reshape(batch, seq, D_MODEL)
```

```python
import functools

import jax
import jax.numpy as jnp
from jax import lax
from jax.experimental import pallas as pl
from jax.experimental.pallas import tpu as pltpu

F32 = jnp.float32
BF16 = jnp.bfloat16

D_MODEL = 1024
DEPTH = 2
BRANCH_W = 512
N_BRANCH = 3
GM_GROUPS = 4
GM_CHUNK = 128
HEADS = 4
HEAD_D = 128
RET_CHUNK = 128
RET_UNROLL = 4
GDN_CHUNK = 64
PHASE_A_CHUNKS = 4
CONV_TILE = 128
ROPE_BASE = 10000.0
EPS = 1e-6
LANES = 128
MAIN_COLS = 11 * BRANCH_W
GM_COLS = 3 * BRANCH_W
RET_COLS = 4 * BRANCH_W
CONV_COLS = 3 * BRANCH_W
PLAIN_COLS = RET_COLS + BRANCH_W
MIX_COLS = PLAIN_COLS + CONV_COLS
COL_RQ, COL_RK, COL_RV, COL_RZ = 0, 1, 2, 3
COL_DZ, COL_DQ, COL_DK, COL_DV = 4, 5, 6, 7
HALO = 8

VMEM_LIMIT = 56 * 1024 * 1024


def _dot(a, b):
    return jnp.dot(a, b, preferred_element_type=F32)


def _dot_nt(a, b):
    return lax.dot_general(a, b, (((1,), (1,)), ((), ())), preferred_element_type=F32)


def _dot_tn(a, b):
    return lax.dot_general(a, b, (((0,), (0,)), ((), ())), preferred_element_type=F32)


def _silu(x):
    return x * jax.nn.sigmoid(x)


def _gelu(x):
    return 0.5 * x * (1.0 + lax.erf(x * 0.7071067811865476))


def _params(*sem):
    return pltpu.CompilerParams(dimension_semantics=sem, vmem_limit_bytes=VMEM_LIMIT)


def _head_slice(h):
    return slice(h * HEAD_D, (h + 1) * HEAD_D)


def _rows(start, size):
    return pl.ds(start if isinstance(start, int) else pl.multiple_of(start, size), size)


def _inproj_body(tiles_per_seq, x_ref, xp_ref, xn_ref, g_ref, w_ref, wab_ref, lng_ref, lnb_ref, ws_ref, bias_ref,
                 cw_ref, alog_ref, dtb_ref, o_ref, gb_ref, gcf_ref, gcb_ref, ya_ref, pc_s):
    tm = x_ref.shape[0]

    def normed(x):
        return (x * lax.rsqrt(jnp.mean(x * x, axis=-1, keepdims=True) + EPS) * g_ref[...]).astype(BF16)

    hb = normed(x_ref[...])
    hb_ext = jnp.concatenate([hb, normed(jnp.concatenate([xp_ref[...], xn_ref[...]], axis=0))], axis=0)
    ab = _dot(hb, wab_ref[...])
    gm = _dot(hb, w_ref[:, 0:GM_COLS])
    o_ref[:, 0:RET_COLS] = _dot(hb, w_ref[:, GM_COLS:GM_COLS + RET_COLS]).astype(o_ref.dtype)
    o_ref[:, RET_COLS:PLAIN_COLS] = _dot(hb, w_ref[:, MAIN_COLS - BRANCH_W:MAIN_COLS]).astype(o_ref.dtype)
    pc = _dot(hb_ext, w_ref[:, GM_COLS + RET_COLS:GM_COLS + RET_COLS + CONV_COLS])

    t_in_seq = pl.program_id(0) % tiles_per_seq
    keep_before = jnp.where(t_in_seq != 0, 1.0, 0.0)
    keep_after = jnp.where(t_in_seq != tiles_per_seq - 1, 1.0, 0.0)
    for s in range(CONV_COLS // LANES):
        cs = slice(s * LANES, (s + 1) * LANES)
        pc_s[s, 0:HALO, :] = pc[tm:tm + HALO, cs] * keep_before
        pc_s[s, HALO:HALO + tm, :] = pc[0:tm, cs]
        pc_s[s, HALO + tm:2 * HALO + tm, :] = pc[tm + HALO:tm + 2 * HALO, cs] * keep_after
    for s in range(CONV_COLS // LANES):
        cs = slice(s * LANES, (s + 1) * LANES)
        w = cw_ref[:, cs]
        piece = s // HEADS
        for j in range(tm // CONV_TILE):
            base = HALO + j * CONV_TILE
            y = _silu(pc_s[s, base - 1:base - 1 + CONV_TILE, :] * w[0:1, :]
                      + pc_s[s, base:base + CONV_TILE, :] * w[1:2, :]
                      + pc_s[s, base + 1:base + 1 + CONV_TILE, :] * w[2:3, :])
            if piece < 2:
                scale = HEAD_D ** -0.5 if piece == 0 else 1.0
                y = y * (lax.rsqrt(jnp.sum(y * y, axis=-1, keepdims=True) + EPS) * scale)
            o_ref[j * CONV_TILE:(j + 1) * CONV_TILE, PLAIN_COLS + s * LANES:PLAIN_COLS + (s + 1) * LANES] = (
                y.astype(o_ref.dtype))

    lane = lax.broadcasted_iota(jnp.int32, (tm, LANES), 1)
    pos = lax.broadcasted_iota(jnp.int32, (tm, LANES), 0) & (GDN_CHUNK - 1)
    g_all = jnp.where((lane & 7) < HEADS,
                      -jnp.exp(alog_ref[...]) * jax.nn.softplus(ab + dtb_ref[...]),
                      jax.nn.sigmoid(ab))
    gb_ref[...] = g_all
    gf = g_all
    gb = g_all
    s = 1
    while s < GDN_CHUNK:
        gf = gf + jnp.where(pos >= s, pltpu.roll(gf, s, 0), 0.0)
        gb = gb + jnp.where(pos < GDN_CHUNK - s, pltpu.roll(gb, tm - s, 0), 0.0)
        s *= 2
    gcf_ref[...] = gf
    gcb_ref[...] = gb

    for c in range(tm // GM_CHUNK):
        r = slice(c * GM_CHUNK, (c + 1) * GM_CHUNK)
        u = _gelu(gm[r, 0:BRANCH_W])
        v = _gelu(gm[r, BRANCH_W:2 * BRANCH_W])
        z = gm[r, 2 * BRANCH_W:3 * BRANCH_W]
        mu = jnp.mean(v, axis=-1, keepdims=True)
        vc = v - mu
        var = jnp.mean(vc * vc, axis=-1, keepdims=True)
        vn = (vc * lax.rsqrt(var + EPS) * lng_ref[...] + lnb_ref[...]).astype(BF16)
        gate = u * _silu(z)
        for g in range(GM_GROUPS):
            cs = slice(g * LANES, (g + 1) * LANES)
            mixed = _dot(ws_ref[g], vn[:, cs]) + bias_ref[:, cs]
            ya_ref[r, cs] = (gate[:, cs] * mixed).astype(ya_ref.dtype)


def _in_proj(x, g, w_main, w_ab, layer, ln_g, ln_b, w_s, bias, conv_w, alog_row, dtb_row, seq, tm=512):
    tokens = x.shape[0]
    halo_blocks = tm // HALO
    row = lambda w: pl.BlockSpec((tm, w), lambda i: (i, 0))
    full = lambda shape, **kw: pl.BlockSpec(shape, lambda i: (0,) * len(shape), **kw)
    before = pl.BlockSpec((HALO, D_MODEL), lambda i: (jnp.maximum(i * halo_blocks - 1, 0), 0))
    after = pl.BlockSpec((HALO, D_MODEL), lambda i: (jnp.minimum((i + 1) * halo_blocks, tokens // HALO - 1), 0))
    return pl.pallas_call(
        functools.partial(_inproj_body, seq // tm),
        grid=(tokens // tm,),
        in_specs=[row(D_MODEL), before, after, full((1, D_MODEL)),
                  pl.BlockSpec((None, D_MODEL, MAIN_COLS), lambda i: (layer, 0, 0), pipeline_mode=pl.Buffered(1)),
                  pl.BlockSpec((None, D_MODEL, LANES), lambda i: (layer, 0, 0)),
                  full((1, BRANCH_W)), full((1, BRANCH_W)), full((GM_GROUPS, GM_CHUNK, GM_CHUNK)),
                  full((GM_CHUNK, BRANCH_W)), full((3, CONV_COLS)), full((1, LANES)), full((1, LANES))],
        out_specs=[row(MIX_COLS), row(LANES), row(LANES), row(LANES), row(BRANCH_W)],
        out_shape=[
            jax.ShapeDtypeStruct((tokens, MIX_COLS), BF16),
            jax.ShapeDtypeStruct((tokens, LANES), F32),
            jax.ShapeDtypeStruct((tokens, LANES), F32),
            jax.ShapeDtypeStruct((tokens, LANES), F32),
            jax.ShapeDtypeStruct((tokens, BRANCH_W), BF16),
        ],
        scratch_shapes=[pltpu.VMEM((CONV_COLS // LANES, tm + 2 * HALO, LANES), F32)],
        compiler_params=_params("arbitrary"),
        name="in_proj",
    )(x, x, x, g, w_main, w_ab, ln_g, ln_b, w_s, bias, conv_w, alog_row, dtb_row)


def _ret_body(q_ref, k_ref, v_ref, z_ref, cos_ref, sin_ref, dl_ref, ng_ref, y_ref, qr_s, kr_s, kv_s):
    C = RET_CHUNK
    n_chunks = q_ref.shape[0] // C
    row = lax.broadcasted_iota(jnp.int32, (C, C), 0).astype(F32)
    colm = lax.broadcasted_iota(jnp.int32, (C, C), 1).astype(F32)
    rel = row - colm

    decs = []
    for h in range(HEADS):
        lf = jnp.broadcast_to(jax.nn.log_sigmoid(dl_ref[0, h])[0:1, :], (C, C))
        lb = jnp.broadcast_to(jax.nn.log_sigmoid(dl_ref[1, h])[0:1, :], (C, C))
        decs.append(dict(
            mask=jnp.exp(jnp.where(rel >= 0, lf * rel, -lb * rel)),
            qdf=jnp.exp(lf * (row + 1.0)), kdf=jnp.exp(lf * (C - 1.0 - row)),
            qdb=jnp.exp(lb * (C - row)), kdb=jnp.exp(lb * row),
            cf=jnp.exp(lf * C), cb=jnp.exp(lb * C)))

    def phase1(it, carry):
        cs = [it * RET_UNROLL + t for t in range(RET_UNROLL)]
        rows = [_rows(c * C, C) for c in cs]
        items = [(t, h) for t in range(RET_UNROLL) for h in range(HEADS)]
        cos = [cos_ref[r, :] for r in rows]
        sin = [sin_ref[r, :] for r in rows]
        qs = [q_ref[rows[t], _head_slice(h)].astype(F32) for t, h in items]
        ks = [k_ref[rows[t], _head_slice(h)].astype(F32) for t, h in items]
        qr = [q * cos[t] + pltpu.roll(q, HEAD_D // 2, 1) * sin[t] for (t, h), q in zip(items, qs)]
        kr = [(k * cos[t] + pltpu.roll(k, HEAD_D // 2, 1) * sin[t]) * (HEAD_D ** -0.5) for (t, h), k in zip(items, ks)]
        kcat = [jnp.concatenate([k * decs[h]["kdf"], k * decs[h]["kdb"]], axis=1).astype(BF16)
                for (t, h), k in zip(items, kr)]
        kvs = [_dot_tn(kc, v_ref[rows[t], _head_slice(h)]) for (t, h), kc in zip(items, kcat)]
        for n, (t, h) in enumerate(items):
            qr_s[rows[t], _head_slice(h)] = qr[n].astype(qr_s.dtype)
            kr_s[rows[t], _head_slice(h)] = kr[n].astype(kr_s.dtype)
            kv_s[h, cs[t]] = kvs[n]
        return carry

    lax.fori_loop(0, n_chunks // RET_UNROLL, phase1, 0)

    for h in range(HEADS):
        cf, cb = decs[h]["cf"], decs[h]["cb"]

        def scan(i, carry, h=h, cf=cf, cb=cb):
            sf, sb = carry
            j = n_chunks - 1 - i
            upd_f = kv_s[h, i, 0:HEAD_D, :]
            kv_s[h, i, 0:HEAD_D, :] = sf
            upd_b = kv_s[h, j, HEAD_D:2 * HEAD_D, :]
            kv_s[h, j, HEAD_D:2 * HEAD_D, :] = sb
            return sf * cf + upd_f, sb * cb + upd_b

        zero = jnp.zeros((HEAD_D, HEAD_D), F32)
        lax.fori_loop(0, n_chunks, scan, (zero, zero))

    def phase3(it, carry):
        cs = [it * RET_UNROLL + t for t in range(RET_UNROLL)]
        rows = [_rows(c * C, C) for c in cs]
        items = [(t, h) for t in range(RET_UNROLL) for h in range(HEADS)]
        qr = [qr_s[rows[t], _head_slice(h)] for t, h in items]
        sc = [_dot_nt(q, kr_s[rows[t], _head_slice(h)]) * decs[h]["mask"] for (t, h), q in zip(items, qr)]
        lhs, rhs = [], []
        for n, (t, h) in enumerate(items):
            q = qr[n].astype(F32)
            lhs.append(jnp.concatenate([sc[n], q * decs[h]["qdf"], q * decs[h]["qdb"]], axis=1).astype(BF16))
            rhs.append(jnp.concatenate([v_ref[rows[t], _head_slice(h)], kv_s[h, cs[t]].astype(BF16)], axis=0))
        outs = [_dot(a, b) for a, b in zip(lhs, rhs)]
        mus = [jnp.mean(o, axis=-1, keepdims=True) for o in outs]
        ocs = [o - mu for o, mu in zip(outs, mus)]
        var = [jnp.mean(oc * oc, axis=-1, keepdims=True) for oc in ocs]
        for n, (t, h) in enumerate(items):
            hs = _head_slice(h)
            on = ocs[n] * lax.rsqrt(var[n] + EPS) * ng_ref[:, hs]
            y_ref[rows[t], hs] = (on * _silu(z_ref[rows[t], hs].astype(F32))).astype(y_ref.dtype)
        return carry

    lax.fori_loop(0, n_chunks // RET_UNROLL, phase3, 0)


def _retention(proj, cos, sin, dl, ng):
    seq = cos.shape[0]
    batch = proj.shape[0] // seq
    col = lambda c: pl.BlockSpec((seq, BRANCH_W), lambda b, c=c: (b, c))
    full = lambda shape: pl.BlockSpec(shape, lambda b: (0,) * len(shape))
    return pl.pallas_call(
        _ret_body,
        grid=(batch,),
        in_specs=[col(COL_RQ), col(COL_RK), col(COL_RV), col(COL_RZ),
                  full((seq, HEAD_D)), full((seq, HEAD_D)), full((2, HEADS, 8, LANES)), full((1, BRANCH_W))],
        out_specs=pl.BlockSpec((seq, BRANCH_W), lambda b: (b, 0)),
        out_shape=jax.ShapeDtypeStruct((batch * seq, BRANCH_W), BF16),
        scratch_shapes=[pltpu.VMEM((seq, BRANCH_W), BF16), pltpu.VMEM((seq, BRANCH_W), BF16),
                        pltpu.VMEM((HEADS, seq // RET_CHUNK, 2 * HEAD_D, HEAD_D), F32)],
        compiler_params=_params("arbitrary"),
        name="retention",
    )(proj, proj, proj, proj, cos, sin, dl, ng)


def _gdn_body(qn_s, kn_s, vn_s, z_ref, g_s, gcf_s, gcb_s, ng_ref, y_ref, u_s, w_s, sc_s, tri_s, o_s, st_s):
    C = GDN_CHUNK
    seq = qn_s.shape[0]
    n_chunks = seq // C

    ri = lax.broadcasted_iota(jnp.int32, (C, C), 0)
    ci = lax.broadcasted_iota(jnp.int32, (C, C), 1)
    eye = (ri == ci).astype(F32)
    masks = ((ri >= ci, ri > ci), (ri <= ci, ri < ci))
    levels = [((ri >> (j + 1)) == (ci >> (j + 1))) & ((ri >> j) != (ci >> j)) for j in range(6)]
    chains = [(d, h) for d in range(2) for h in range(HEADS)]
    gc_refs = (gcf_s, gcb_s)

    U = PHASE_A_CHUNKS
    n_groups = n_chunks // U

    work = [(t, d, h) for t in range(U) for d, h in chains]

    def group_rows(grp):
        return ([_rows((grp * U + t) * C, C) for t in range(U)],
                [_rows((n_chunks - 1 - grp * U - t) * C, C) for t in range(U)])

    def prep(grp):
        rows = group_rows(grp)
        gcs = {(t, d): gc_refs[d][rows[d][t], :] for t in range(U) for d in range(2)}
        gts = {key: g.T for key, g in gcs.items()}
        gbeta = {(t, d): g_s[rows[d][t], :] for t in range(U) for d in range(2)}
        for t, d, h in work:
            r, hs = rows[d][t], _head_slice(h)
            incl, strict = masks[d]
            gl, bl = 8 * d + h, 8 * d + HEADS + h
            k16 = kn_s[r, hs]
            kk = _dot_nt(jnp.concatenate([k16, qn_s[r, hs]], axis=0), k16)
            gcol = gcs[t, d][:, gl:gl + 1]
            beta = gbeta[t, d][:, bl:bl + 1]
            grow = jnp.broadcast_to(gts[t, d][gl:gl + 1, :], (C, C))
            decay = jnp.exp(jnp.where(incl, gcol - grow, -jnp.inf))
            tri_s[d, h, r, :] = jnp.where(strict, beta * kk[:C] * decay, 0.0).astype(tri_s.dtype)
            sc_s[d, h, r, :] = (kk[C:] * decay).astype(sc_s.dtype)

    def solve(grp):
        rows = group_rows(grp)
        zero16 = jnp.zeros((C, C), BF16)
        tri = [tri_s[d, h, rows[d][t], :] for t, d, h in work]
        inv = [eye - jnp.where(levels[0], x, zero16).astype(F32) for x in tri]
        for j in range(1, 6):
            m16 = [jnp.where(levels[j], x, zero16) for x in tri]
            i16 = [p.astype(BF16) for p in inv]
            mi = [_dot(m, p).astype(BF16) for m, p in zip(m16, i16)]
            inv = [p - _dot(p16, x) for p, p16, x in zip(inv, i16, mi)]
        gcs = {(t, d): gc_refs[d][rows[d][t], :] for t in range(U) for d in range(2)}
        gbeta = {(t, d): g_s[rows[d][t], :] for t in range(U) for d in range(2)}
        for (t, d, h), p in zip(work, inv):
            r, hs = rows[d][t], _head_slice(h)
            gl, bl = 8 * d + h, 8 * d + HEADS + h
            gcol = jnp.broadcast_to(gcs[t, d][:, gl:gl + 1], (C, HEAD_D))
            beta = jnp.broadcast_to(gbeta[t, d][:, bl:bl + 1], (C, HEAD_D))
            kbeta = kn_s[r, hs].astype(F32) * beta
            rhs = jnp.concatenate([vn_s[r, hs].astype(F32) * beta, kbeta * jnp.exp(gcol)], axis=1).astype(BF16)
            sol = _dot(p.astype(BF16), rhs)
            u_s[d, r, hs] = sol[:, :HEAD_D].astype(u_s.dtype)
            w_s[d, r, hs] = sol[:, HEAD_D:].astype(w_s.dtype)

    def recur(grp, final):
        rows = group_rows(grp)
        st = [st_s[n] for n in range(len(chains))]
        for t in range(U):
            gcs = [gc_refs[d][rows[d][t], :] for d in range(2)]
            lhs, kd, egl = [], [], []
            for d, h in chains:
                r, hs = rows[d][t], _head_slice(h)
                gl = 8 * d + h
                gcol = jnp.broadcast_to(gcs[d][:, gl:gl + 1], (C, HEAD_D))
                last = C - 1 if d == 0 else 0
                glast = gcol[last:last + 1, :]
                qe = (qn_s[r, hs].astype(F32) * jnp.exp(gcol)).astype(BF16)
                lhs.append(jnp.concatenate([w_s[d, r, hs], qe], axis=0))
                kd.append((kn_s[r, hs].astype(F32) * jnp.exp(glast - gcol)).astype(BF16))
                egl.append(jnp.exp(glast))
            ws = [_dot(a, s_.astype(BF16)) for a, s_ in zip(lhs, st)]
            vnew = [(u_s[d, rows[d][t], _head_slice(h)].astype(F32) - x[:C]).astype(BF16)
                    for (d, h), x in zip(chains, ws)]
            outs = [ws[n][C:] + _dot(sc_s[d, h, rows[d][t], :], vnew[n]) for n, (d, h) in enumerate(chains)]
            for n, (d, h) in enumerate(chains):
                r, hs = rows[d][t], _head_slice(h)
                if final:
                    o = outs[n] + o_s[r, hs].astype(F32)
                    on = o * lax.rsqrt(jnp.mean(o * o, axis=-1, keepdims=True) + EPS) * ng_ref[...]
                    y_ref[r, hs] = (on * _silu(z_ref[r, hs].astype(F32))).astype(y_ref.dtype)
                else:
                    o_s[r, hs] = outs[n].astype(o_s.dtype)
            st = [s_ * e + _dot_tn(a, x) for s_, e, a, x in zip(st, egl, kd, vnew)]
        for n, s_ in enumerate(st):
            st_s[n] = s_

    half = n_groups // 2
    st_s[...] = jnp.zeros_like(st_s)
    prep(0)
    solve(0)
    prep(1)

    def pipelined(final, grp, carry):
        recur(grp - 2, final)
        solve(grp - 1)
        prep(grp)
        return carry

    lax.fori_loop(2, half + 2, functools.partial(pipelined, False), 0)
    lax.fori_loop(half + 2, n_groups, functools.partial(pipelined, True), 0)
    recur(n_groups - 2, True)
    solve(n_groups - 1)
    recur(n_groups - 1, True)


def _gdn(proj, g_all, gcf, gcb, ng, seq):
    batch = proj.shape[0] // seq
    col = lambda c: pl.BlockSpec((seq, BRANCH_W), lambda b, c=c: (b, c))
    lanes = pl.BlockSpec((seq, LANES), lambda b: (b, 0))
    full = lambda shape: pl.BlockSpec(shape, lambda b: (0,) * len(shape))
    per_chain = pltpu.VMEM((2, HEADS, seq, GDN_CHUNK), BF16)
    return pl.pallas_call(
        _gdn_body,
        grid=(batch,),
        in_specs=[col(COL_DQ), col(COL_DK), col(COL_DV), col(COL_DZ), lanes, lanes, lanes, full((1, HEAD_D))],
        out_specs=pl.BlockSpec((seq, BRANCH_W), lambda b: (b, 0)),
        out_shape=jax.ShapeDtypeStruct((batch * seq, BRANCH_W), BF16),
        scratch_shapes=[pltpu.VMEM((2, seq, BRANCH_W), BF16), pltpu.VMEM((2, seq, BRANCH_W), BF16),
                        per_chain, per_chain, pltpu.VMEM((seq, BRANCH_W), BF16),
                        pltpu.VMEM((2 * HEADS, HEAD_D, HEAD_D), F32)],
        compiler_params=_params("arbitrary"),
        name="gdn",
    )(proj, proj, proj, proj, g_all, gcf, gcb, ng)


def _merge_body(final, x_ref, g_ref, ya_ref, yb_ref, yc_ref, wg_ref, wbo_ref, wo_ref, *rest):
    if final:
        fg_ref, o_ref = rest
    else:
        (o_ref,) = rest
    x = x_ref[...]
    h = (x * lax.rsqrt(jnp.mean(x * x, axis=-1, keepdims=True) + EPS) * g_ref[...]).astype(BF16)
    merged = None
    for n, y_ref in enumerate((ya_ref, yb_ref, yc_ref)):
        gate = jax.nn.sigmoid(_dot(h, wg_ref[:, n * D_MODEL:(n + 1) * D_MODEL]))
        term = gate * _dot(y_ref[...], wbo_ref[n])
        merged = term if merged is None else merged + term
    out = x + _dot(merged.astype(BF16), wo_ref[...])
    if final:
        out = out * lax.rsqrt(jnp.mean(out * out, axis=-1, keepdims=True) + EPS) * fg_ref[...]
    o_ref[...] = out


def _merge(x, g, ya, yb, yc, wg, wbo, wo, layer, final_g=None, tm=512):
    tokens = x.shape[0]
    final = final_g is not None
    row = lambda w: pl.BlockSpec((tm, w), lambda i: (i, 0))
    full = lambda shape: pl.BlockSpec(shape, lambda i: (0,) * len(shape))
    of_layer = lambda shape: pl.BlockSpec((None,) + shape, lambda i: (layer,) + (0,) * len(shape))
    in_specs = [row(D_MODEL), full((1, D_MODEL)), row(BRANCH_W), row(BRANCH_W), row(BRANCH_W),
                of_layer((D_MODEL, N_BRANCH * D_MODEL)), of_layer((N_BRANCH, BRANCH_W, D_MODEL)),
                of_layer((D_MODEL, D_MODEL))]
    args = [x, g, ya, yb, yc, wg, wbo, wo]
    if final:
        in_specs.append(full((1, D_MODEL)))
        args.append(final_g)
    return pl.pallas_call(
        functools.partial(_merge_body, final),
        grid=(tokens // tm,),
        in_specs=in_specs,
        out_specs=row(D_MODEL),
        out_shape=jax.ShapeDtypeStruct((tokens, D_MODEL), F32),
        compiler_params=_params("arbitrary"),
        name="merge_final" if final else "merge",
    )(*args)


def _rope_tables(seq):
    inv = ROPE_BASE ** (-jnp.arange(0, HEAD_D, 2, dtype=F32) / HEAD_D)
    ang = jnp.arange(seq, dtype=F32)[:, None] * inv[None, :]
    cos = jnp.cos(ang)
    sin = jnp.sin(ang)
    return jnp.concatenate([cos, cos], axis=-1), jnp.concatenate([-sin, sin], axis=-1)


def _lane_row(p):
    row = jnp.zeros((LANES,), F32)
    row = row.at[0:HEADS].set(p[0]).at[2 * HEADS:3 * HEADS].set(p[1])
    return row[None, :]


def kernel(x, norm_g, w_in, gm_ln_g, gm_ln_b, gm_w_s, gm_b_s, ret_decay_logit, ret_norm_g, gdn_conv_w, gdn_a_log, gdn_dt_bias, gdn_norm_g, w_gate, w_branch_out, w_out, final_norm_g):
    batch, seq, _ = x.shape
    xf = x.reshape(batch * seq, D_MODEL)
    cos, sin = _rope_tables(seq)
    w_main = w_in[:, :, :MAIN_COLS].astype(BF16)
    w_ab = jnp.pad(w_in[:, :, MAIN_COLS:], ((0, 0), (0, 0), (0, LANES - 4 * HEADS))).astype(BF16)
    wg = w_gate.reshape(DEPTH, D_MODEL, N_BRANCH * D_MODEL).astype(BF16)
    wbo = w_branch_out.astype(BF16)
    wo = w_out.astype(BF16)
    for l in range(DEPTH):
        g_row = norm_g[l][None, :]
        bias = jnp.repeat(gm_b_s[l].T, GM_CHUNK, axis=1)
        proj, g_all, gcf, gcb, ya = _in_proj(
            xf, g_row, w_main, w_ab, l, gm_ln_g[l][None, :], gm_ln_b[l][None, :], gm_w_s[l].astype(BF16), bias,
            gdn_conv_w[l], _lane_row(gdn_a_log[l]), _lane_row(gdn_dt_bias[l]), seq)

        dl = jnp.broadcast_to(ret_decay_logit[l][:, :, None, None], (2, HEADS, 8, LANES))
        yb = _retention(proj, cos, sin, dl, ret_norm_g[l][None, :])

        yc = _gdn(proj, g_all, gcf, gcb, gdn_norm_g[l][None, :], seq)

        xf = _merge(xf, g_row, ya, yb, yc, wg, wbo, wo, l, final_norm_g[None, :] if l == DEPTH - 1 else None)
    return xf.reshape(batch, seq, D_MODEL)
```

```python
import functools

import jax
import jax.numpy as jnp
from jax import lax
from jax.experimental import pallas as pl
from jax.experimental.pallas import tpu as pltpu

F32 = jnp.float32
BF16 = jnp.bfloat16

D_MODEL = 1024
DEPTH = 2
BRANCH_W = 512
N_BRANCH = 3
GM_GROUPS = 4
GM_CHUNK = 128
HEADS = 4
HEAD_D = 128
RET_CHUNK = 128
RET_UNROLL = 4
GDN_CHUNK = 64
PHASE_A_CHUNKS = 4
CONV_TILE = 128
ROPE_BASE = 10000.0
EPS = 1e-6
LANES = 128
MAIN_COLS = 11 * BRANCH_W
GM_COLS = 3 * BRANCH_W
RET_COLS = 4 * BRANCH_W
CONV_COLS = 3 * BRANCH_W
PLAIN_COLS = RET_COLS + BRANCH_W
MIX_COLS = PLAIN_COLS + CONV_COLS
COL_RQ, COL_RK, COL_RV, COL_RZ = 0, 1, 2, 3
COL_DZ, COL_DQ, COL_DK, COL_DV = 4, 5, 6, 7
HALO = 8

VMEM_LIMIT = 56 * 1024 * 1024


def _dot(a, b):
    return jnp.dot(a, b, preferred_element_type=F32)


def _dot_nt(a, b):
    return lax.dot_general(a, b, (((1,), (1,)), ((), ())), preferred_element_type=F32)


def _dot_tn(a, b):
    return lax.dot_general(a, b, (((0,), (0,)), ((), ())), preferred_element_type=F32)


def _silu(x):
    return x * jax.nn.sigmoid(x)


def _gelu(x):
    return 0.5 * x * (1.0 + lax.erf(x * 0.7071067811865476))


def _params(*sem):
    return pltpu.CompilerParams(dimension_semantics=sem, vmem_limit_bytes=VMEM_LIMIT)


def _head_slice(h):
    return slice(h * HEAD_D, (h + 1) * HEAD_D)


def _rows(start, size):
    return pl.ds(start if isinstance(start, int) else pl.multiple_of(start, size), size)


def _inproj_body(tiles_per_seq, x_ref, xp_ref, xn_ref, g_ref, w_ref, wab_ref, lng_ref, lnb_ref, ws_ref, bias_ref,
                 cw_ref, alog_ref, dtb_ref, o_ref, gb_ref, gcf_ref, gcb_ref, ya_ref, pc_s):
    tm = x_ref.shape[0]

    def normed(x):
        return (x * lax.rsqrt(jnp.mean(x * x, axis=-1, keepdims=True) + EPS) * g_ref[...]).astype(BF16)

    hb = normed(x_ref[...])
    hb_ext = jnp.concatenate([hb, normed(jnp.concatenate([xp_ref[...], xn_ref[...]], axis=0))], axis=0)
    ab = _dot(hb, wab_ref[...])
    gm = _dot(hb, w_ref[:, 0:GM_COLS])
    o_ref[:, 0:RET_COLS] = _dot(hb, w_ref[:, GM_COLS:GM_COLS + RET_COLS]).astype(o_ref.dtype)
    o_ref[:, RET_COLS:PLAIN_COLS] = _dot(hb, w_ref[:, MAIN_COLS - BRANCH_W:MAIN_COLS]).astype(o_ref.dtype)
    pc = _dot(hb_ext, w_ref[:, GM_COLS + RET_COLS:GM_COLS + RET_COLS + CONV_COLS])

    t_in_seq = pl.program_id(0) % tiles_per_seq
    keep_before = jnp.where(t_in_seq != 0, 1.0, 0.0)
    keep_after = jnp.where(t_in_seq != tiles_per_seq - 1, 1.0, 0.0)
    for s in range(CONV_COLS // LANES):
        cs = slice(s * LANES, (s + 1) * LANES)
        pc_s[s, 0:HALO, :] = pc[tm:tm + HALO, cs] * keep_before
        pc_s[s, HALO:HALO + tm, :] = pc[0:tm, cs]
        pc_s[s, HALO + tm:2 * HALO + tm, :] = pc[tm + HALO:tm + 2 * HALO, cs] * keep_after
    for s in range(CONV_COLS // LANES):
        cs = slice(s * LANES, (s + 1) * LANES)
        w = cw_ref[:, cs]
        piece = s // HEADS
        for j in range(tm // CONV_TILE):
            base = HALO + j * CONV_TILE
            y = _silu(pc_s[s, base - 1:base - 1 + CONV_TILE, :] * w[0:1, :]
                      + pc_s[s, base:base + CONV_TILE, :] * w[1:2, :]
                      + pc_s[s, base + 1:base + 1 + CONV_TILE, :] * w[2:3, :])
            if piece < 2:
                scale = HEAD_D ** -0.5 if piece == 0 else 1.0
                y = y * (lax.rsqrt(jnp.sum(y * y, axis=-1, keepdims=True) + EPS) * scale)
            o_ref[j * CONV_TILE:(j + 1) * CONV_TILE, PLAIN_COLS + s * LANES:PLAIN_COLS + (s + 1) * LANES] = (
                y.astype(o_ref.dtype))

    lane = lax.broadcasted_iota(jnp.int32, (tm, LANES), 1)
    pos = lax.broadcasted_iota(jnp.int32, (tm, LANES), 0) & (GDN_CHUNK - 1)
    g_all = jnp.where((lane & 7) < HEADS,
                      -jnp.exp(alog_ref[...]) * jax.nn.softplus(ab + dtb_ref[...]),
                      jax.nn.sigmoid(ab))
    gb_ref[...] = g_all
    gf = g_all
    gb = g_all
    s = 1
    while s < GDN_CHUNK:
        gf = gf + jnp.where(pos >= s, pltpu.roll(gf, s, 0), 0.0)
        gb = gb + jnp.where(pos < GDN_CHUNK - s, pltpu.roll(gb, tm - s, 0), 0.0)
        s *= 2
    gcf_ref[...] = gf
    gcb_ref[...] = gb

    for c in range(tm // GM_CHUNK):
        r = slice(c * GM_CHUNK, (c + 1) * GM_CHUNK)
        u = _gelu(gm[r, 0:BRANCH_W])
        v = _gelu(gm[r, BRANCH_W:2 * BRANCH_W])
        z = gm[r, 2 * BRANCH_W:3 * BRANCH_W]
        mu = jnp.mean(v, axis=-1, keepdims=True)
        vc = v - mu
        var = jnp.mean(vc * vc, axis=-1, keepdims=True)
        vn = (vc * lax.rsqrt(var + EPS) * lng_ref[...] + lnb_ref[...]).astype(BF16)
        gate = u * _silu(z)
        for g in range(GM_GROUPS):
            cs = slice(g * LANES, (g + 1) * LANES)
            mixed = _dot(ws_ref[g], vn[:, cs]) + bias_ref[:, cs]
            ya_ref[r, cs] = (gate[:, cs] * mixed).astype(ya_ref.dtype)


def _in_proj(x, g, w_main, w_ab, layer, ln_g, ln_b, w_s, bias, conv_w, alog_row, dtb_row, seq, tm=512):
    tokens = x.shape[0]
    halo_blocks = tm // HALO
    row = lambda w: pl.BlockSpec((tm, w), lambda i: (i, 0))
    full = lambda shape, **kw: pl.BlockSpec(shape, lambda i: (0,) * len(shape), **kw)
    before = pl.BlockSpec((HALO, D_MODEL), lambda i: (jnp.maximum(i * halo_blocks - 1, 0), 0))
    after = pl.BlockSpec((HALO, D_MODEL), lambda i: (jnp.minimum((i + 1) * halo_blocks, tokens // HALO - 1), 0))
    return pl.pallas_call(
        functools.partial(_inproj_body, seq // tm),
        grid=(tokens // tm,),
        in_specs=[row(D_MODEL), before, after, full((1, D_MODEL)),
                  pl.BlockSpec((None, D_MODEL, MAIN_COLS), lambda i: (layer, 0, 0), pipeline_mode=pl.Buffered(1)),
                  pl.BlockSpec((None, D_MODEL, LANES), lambda i: (layer, 0, 0)),
                  full((1, BRANCH_W)), full((1, BRANCH_W)), full((GM_GROUPS, GM_CHUNK, GM_CHUNK)),
                  full((GM_CHUNK, BRANCH_W)), full((3, CONV_COLS)), full((1, LANES)), full((1, LANES))],
        out_specs=[row(MIX_COLS), row(LANES), row(LANES), row(LANES), row(BRANCH_W)],
        out_shape=[
            jax.ShapeDtypeStruct((tokens, MIX_COLS), BF16),
            jax.ShapeDtypeStruct((tokens, LANES), F32),
            jax.ShapeDtypeStruct((tokens, LANES), F32),
            jax.ShapeDtypeStruct((tokens, LANES), F32),
            jax.ShapeDtypeStruct((tokens, BRANCH_W), BF16),
        ],
        scratch_shapes=[pltpu.VMEM((CONV_COLS // LANES, tm + 2 * HALO, LANES), F32)],
        compiler_params=_params("arbitrary"),
        name="in_proj",
    )(x, x, x, g, w_main, w_ab, ln_g, ln_b, w_s, bias, conv_w, alog_row, dtb_row)


def _ret_body(q_ref, k_ref, v_ref, cos_ref, sin_ref, dl_ref, qr_ref, kr_ref, st_ref, dec_ref, kv_s):
    C = RET_CHUNK
    n_chunks = q_ref.shape[0] // C
    row = lax.broadcasted_iota(jnp.int32, (C, C), 0).astype(F32)
    colm = lax.broadcasted_iota(jnp.int32, (C, C), 1).astype(F32)
    rel = row - colm

    decs = []
    for h in range(HEADS):
        lf = jnp.broadcast_to(jax.nn.log_sigmoid(dl_ref[0, h])[0:1, :], (C, C))
        lb = jnp.broadcast_to(jax.nn.log_sigmoid(dl_ref[1, h])[0:1, :], (C, C))
        decs.append(dict(kdf=jnp.exp(lf * (C - 1.0 - row)), kdb=jnp.exp(lb * row),
                         cf=jnp.exp(lf * C), cb=jnp.exp(lb * C)))
        dec_ref[h, 0] = jnp.exp(jnp.where(rel >= 0, lf * rel, -lb * rel))
        dec_ref[h, 1] = jnp.exp(lf * (row + 1.0))
        dec_ref[h, 2] = jnp.exp(lb * (C - row))

    def phase1(it, carry):
        cs = [it * RET_UNROLL + t for t in range(RET_UNROLL)]
        rows = [_rows(c * C, C) for c in cs]
        items = [(t, h) for t in range(RET_UNROLL) for h in range(HEADS)]
        cos = [cos_ref[r, :] for r in rows]
        sin = [sin_ref[r, :] for r in rows]
        qs = [q_ref[rows[t], _head_slice(h)].astype(F32) for t, h in items]
        ks = [k_ref[rows[t], _head_slice(h)].astype(F32) for t, h in items]
        qr = [q * cos[t] + pltpu.roll(q, HEAD_D // 2, 1) * sin[t] for (t, h), q in zip(items, qs)]
        kr = [(k * cos[t] + pltpu.roll(k, HEAD_D // 2, 1) * sin[t]) * (HEAD_D ** -0.5) for (t, h), k in zip(items, ks)]
        kcat = [jnp.concatenate([k * decs[h]["kdf"], k * decs[h]["kdb"]], axis=1).astype(BF16)
                for (t, h), k in zip(items, kr)]
        kvs = [_dot_tn(kc, v_ref[rows[t], _head_slice(h)]) for (t, h), kc in zip(items, kcat)]
        for n, (t, h) in enumerate(items):
            qr_ref[rows[t], _head_slice(h)] = qr[n].astype(qr_ref.dtype)
            kr_ref[rows[t], _head_slice(h)] = kr[n].astype(kr_ref.dtype)
            kv_s[h, cs[t]] = kvs[n]
        return carry

    lax.fori_loop(0, n_chunks // RET_UNROLL, phase1, 0)

    for h in range(HEADS):
        cf, cb = decs[h]["cf"], decs[h]["cb"]

        def scan(i, carry, h=h, cf=cf, cb=cb):
            sf, sb = carry
            j = n_chunks - 1 - i
            st_ref[h, i, 0:HEAD_D, :] = sf.astype(st_ref.dtype)
            st_ref[h, j, HEAD_D:2 * HEAD_D, :] = sb.astype(st_ref.dtype)
            return sf * cf + kv_s[h, i, 0:HEAD_D, :], sb * cb + kv_s[h, j, HEAD_D:2 * HEAD_D, :]

        zero = jnp.zeros((HEAD_D, HEAD_D), F32)
        lax.fori_loop(0, n_chunks, scan, (zero, zero))


def _retention_states(proj, cos, sin, dl):
    seq = cos.shape[0]
    batch = proj.shape[0] // seq
    n_chunks = seq // RET_CHUNK
    col = lambda c: pl.BlockSpec((seq, BRANCH_W), lambda b, c=c: (b, c))
    full = lambda shape: pl.BlockSpec(shape, lambda b: (0,) * len(shape))
    rows = pl.BlockSpec((seq, BRANCH_W), lambda b: (b, 0))
    return pl.pallas_call(
        _ret_body,
        grid=(batch,),
        in_specs=[col(COL_RQ), col(COL_RK), col(COL_RV),
                  full((seq, HEAD_D)), full((seq, HEAD_D)), full((2, HEADS, 8, LANES))],
        out_specs=[rows, rows,
                   pl.BlockSpec((None, HEADS, n_chunks, 2 * HEAD_D, HEAD_D), lambda b: (b, 0, 0, 0, 0)),
                   full((HEADS, 3, RET_CHUNK, RET_CHUNK))],
        out_shape=[jax.ShapeDtypeStruct((batch * seq, BRANCH_W), BF16),
                   jax.ShapeDtypeStruct((batch * seq, BRANCH_W), BF16),
                   jax.ShapeDtypeStruct((batch, HEADS, n_chunks, 2 * HEAD_D, HEAD_D), BF16),
                   jax.ShapeDtypeStruct((HEADS, 3, RET_CHUNK, RET_CHUNK), F32)],
        scratch_shapes=[pltpu.VMEM((HEADS, n_chunks, 2 * HEAD_D, HEAD_D), F32)],
        compiler_params=_params("arbitrary"),
        name="retention",
    )(proj, proj, proj, cos, sin, dl)


def _gdn_body(qn_s, kn_s, vn_s, z_ref, g_s, gcf_s, gcb_s, ng_ref, y_ref, u_s, w_s, sc_s, tri_s, o_s, st_s):
    C = GDN_CHUNK
    seq = qn_s.shape[0]
    n_chunks = seq // C

    ri = lax.broadcasted_iota(jnp.int32, (C, C), 0)
    ci = lax.broadcasted_iota(jnp.int32, (C, C), 1)
    eye = (ri == ci).astype(F32)
    masks = ((ri >= ci, ri > ci), (ri <= ci, ri < ci))
    levels = [((ri >> (j + 1)) == (ci >> (j + 1))) & ((ri >> j) != (ci >> j)) for j in range(6)]
    chains = [(d, h) for d in range(2) for h in range(HEADS)]
    gc_refs = (gcf_s, gcb_s)

    U = PHASE_A_CHUNKS
    n_groups = n_chunks // U

    work = [(t, d, h) for t in range(U) for d, h in chains]

    def group_rows(grp):
        return ([_rows((grp * U + t) * C, C) for t in range(U)],
                [_rows((n_chunks - 1 - grp * U - t) * C, C) for t in range(U)])

    def prep(grp):
        rows = group_rows(grp)
        gcs = {(t, d): gc_refs[d][rows[d][t], :] for t in range(U) for d in range(2)}
        gts = {key: g.T for key, g in gcs.items()}
        gbeta = {(t, d): g_s[rows[d][t], :] for t in range(U) for d in range(2)}
        for t, d, h in work:
            r, hs = rows[d][t], _head_slice(h)
            incl, strict = masks[d]
            gl, bl = 8 * d + h, 8 * d + HEADS + h
            k16 = kn_s[r, hs]
            kk = _dot_nt(jnp.concatenate([k16, qn_s[r, hs]], axis=0), k16)
            gcol = gcs[t, d][:, gl:gl + 1]
            beta = gbeta[t, d][:, bl:bl + 1]
            grow = jnp.broadcast_to(gts[t, d][gl:gl + 1, :], (C, C))
            decay = jnp.exp(jnp.where(incl, gcol - grow, -jnp.inf))
            tri_s[d, h, r, :] = jnp.where(strict, beta * kk[:C] * decay, 0.0).astype(tri_s.dtype)
            sc_s[d, h, r, :] = (kk[C:] * decay).astype(sc_s.dtype)

    def solve(grp):
        rows = group_rows(grp)
        zero16 = jnp.zeros((C, C), BF16)
        tri = [tri_s[d, h, rows[d][t], :] for t, d, h in work]
        inv = [eye - jnp.where(levels[0], x, zero16).astype(F32) for x in tri]
        for j in range(1, 6):
            m16 = [jnp.where(levels[j], x, zero16) for x in tri]
            i16 = [p.astype(BF16) for p in inv]
            mi = [_dot(m, p).astype(BF16) for m, p in zip(m16, i16)]
            inv = [p - _dot(p16, x) for p, p16, x in zip(inv, i16, mi)]
        gcs = {(t, d): gc_refs[d][rows[d][t], :] for t in range(U) for d in range(2)}
        gbeta = {(t, d): g_s[rows[d][t], :] for t in range(U) for d in range(2)}
        for (t, d, h), p in zip(work, inv):
            r, hs = rows[d][t], _head_slice(h)
            gl, bl = 8 * d + h, 8 * d + HEADS + h
            gcol = jnp.broadcast_to(gcs[t, d][:, gl:gl + 1], (C, HEAD_D))
            beta = jnp.broadcast_to(gbeta[t, d][:, bl:bl + 1], (C, HEAD_D))
            kbeta = kn_s[r, hs].astype(F32) * beta
            rhs = jnp.concatenate([vn_s[r, hs].astype(F32) * beta, kbeta * jnp.exp(gcol)], axis=1).astype(BF16)
            sol = _dot(p.astype(BF16), rhs)
            u_s[d, r, hs] = sol[:, :HEAD_D].astype(u_s.dtype)
            w_s[d, r, hs] = sol[:, HEAD_D:].astype(w_s.dtype)

    def recur(grp, final):
        rows = group_rows(grp)
        st = [st_s[n] for n in range(len(chains))]
        for t in range(U):
            gcs = [gc_refs[d][rows[d][t], :] for d in range(2)]
            lhs, kd, egl = [], [], []
            for d, h in chains:
                r, hs = rows[d][t], _head_slice(h)
                gl = 8 * d + h
                gcol = jnp.broadcast_to(gcs[d][:, gl:gl + 1], (C, HEAD_D))
                last = C - 1 if d == 0 else 0
                glast = gcol[last:last + 1, :]
                qe = (qn_s[r, hs].astype(F32) * jnp.exp(gcol)).astype(BF16)
                lhs.append(jnp.concatenate([w_s[d, r, hs], qe], axis=0))
                kd.append((kn_s[r, hs].astype(F32) * jnp.exp(glast - gcol)).astype(BF16))
                egl.append(jnp.exp(glast))
            ws = [_dot(a, s_.astype(BF16)) for a, s_ in zip(lhs, st)]
            vnew = [(u_s[d, rows[d][t], _head_slice(h)].astype(F32) - x[:C]).astype(BF16)
                    for (d, h), x in zip(chains, ws)]
            outs = [ws[n][C:] + _dot(sc_s[d, h, rows[d][t], :], vnew[n]) for n, (d, h) in enumerate(chains)]
            for n, (d, h) in enumerate(chains):
                r, hs = rows[d][t], _head_slice(h)
                if final:
                    o = outs[n] + o_s[r, hs].astype(F32)
                    on = o * lax.rsqrt(jnp.mean(o * o, axis=-1, keepdims=True) + EPS) * ng_ref[...]
                    y_ref[r, hs] = (on * _silu(z_ref[r, hs].astype(F32))).astype(y_ref.dtype)
                else:
                    o_s[r, hs] = outs[n].astype(o_s.dtype)
            st = [s_ * e + _dot_tn(a, x) for s_, e, a, x in zip(st, egl, kd, vnew)]
        for n, s_ in enumerate(st):
            st_s[n] = s_

    half = n_groups // 2
    st_s[...] = jnp.zeros_like(st_s)
    prep(0)
    solve(0)
    prep(1)

    def pipelined(final, grp, carry):
        recur(grp - 2, final)
        solve(grp - 1)
        prep(grp)
        return carry

    lax.fori_loop(2, half + 2, functools.partial(pipelined, False), 0)
    lax.fori_loop(half + 2, n_groups, functools.partial(pipelined, True), 0)
    recur(n_groups - 2, True)
    solve(n_groups - 1)
    recur(n_groups - 1, True)


def _gdn(proj, g_all, gcf, gcb, ng, seq):
    batch = proj.shape[0] // seq
    col = lambda c: pl.BlockSpec((seq, BRANCH_W), lambda b, c=c: (b, c))
    lanes = pl.BlockSpec((seq, LANES), lambda b: (b, 0))
    full = lambda shape: pl.BlockSpec(shape, lambda b: (0,) * len(shape))
    per_chain = pltpu.VMEM((2, HEADS, seq, GDN_CHUNK), BF16)
    return pl.pallas_call(
        _gdn_body,
        grid=(batch,),
        in_specs=[col(COL_DQ), col(COL_DK), col(COL_DV), col(COL_DZ), lanes, lanes, lanes, full((1, HEAD_D))],
        out_specs=pl.BlockSpec((seq, BRANCH_W), lambda b: (b, 0)),
        out_shape=jax.ShapeDtypeStruct((batch * seq, BRANCH_W), BF16),
        scratch_shapes=[pltpu.VMEM((2, seq, BRANCH_W), BF16), pltpu.VMEM((2, seq, BRANCH_W), BF16),
                        per_chain, per_chain, pltpu.VMEM((seq, BRANCH_W), BF16),
                        pltpu.VMEM((2 * HEADS, HEAD_D, HEAD_D), F32)],
        compiler_params=_params("arbitrary"),
        name="gdn",
    )(proj, proj, proj, proj, g_all, gcf, gcb, ng)


def _merge_body(final, x_ref, g_ref, ya_ref, yc_ref, qr_ref, kr_ref, rv_ref, rz_ref, st_ref, dec_ref, rng_ref,
                wg_ref, wbo_ref, wo_ref, *rest):
    if final:
        fg_ref, o_ref, yb_s = rest
    else:
        o_ref, yb_s = rest
    C = RET_CHUNK
    items = [(t, h) for t in range(x_ref.shape[0] // C) for h in range(HEADS)]
    rows = lambda t: slice(t * C, (t + 1) * C)
    qr = [qr_ref[rows(t), _head_slice(h)] for t, h in items]
    sc = [_dot_nt(q, kr_ref[rows(t), _head_slice(h)]) * dec_ref[h, 0] for (t, h), q in zip(items, qr)]
    lhs, rhs = [], []
    for n, (t, h) in enumerate(items):
        q = qr[n].astype(F32)
        lhs.append(jnp.concatenate([sc[n], q * dec_ref[h, 1], q * dec_ref[h, 2]], axis=1).astype(BF16))
        rhs.append(jnp.concatenate([rv_ref[rows(t), _head_slice(h)], st_ref[h, t]], axis=0))
    outs = [_dot(a, b) for a, b in zip(lhs, rhs)]
    mus = [jnp.mean(o, axis=-1, keepdims=True) for o in outs]
    ocs = [o - mu for o, mu in zip(outs, mus)]
    var = [jnp.mean(oc * oc, axis=-1, keepdims=True) for oc in ocs]
    for n, (t, h) in enumerate(items):
        hs = _head_slice(h)
        on = ocs[n] * lax.rsqrt(var[n] + EPS) * rng_ref[:, hs]
        yb_s[rows(t), hs] = (on * _silu(rz_ref[rows(t), hs].astype(F32))).astype(yb_s.dtype)

    x = x_ref[...]
    h = (x * lax.rsqrt(jnp.mean(x * x, axis=-1, keepdims=True) + EPS) * g_ref[...]).astype(BF16)
    merged = None
    for n, y_ref in enumerate((ya_ref, yb_s, yc_ref)):
        gate = jax.nn.sigmoid(_dot(h, wg_ref[:, n * D_MODEL:(n + 1) * D_MODEL]))
        term = gate * _dot(y_ref[...], wbo_ref[n])
        merged = term if merged is None else merged + term
    out = x + _dot(merged.astype(BF16), wo_ref[...])
    if final:
        out = out * lax.rsqrt(jnp.mean(out * out, axis=-1, keepdims=True) + EPS) * fg_ref[...]
    o_ref[...] = out


def _merge(x, g, ya, yc, proj, qr, kr, st, dec, ret_ng, wg, wbo, wo, layer, seq, final_g=None, tm=512):
    tokens = x.shape[0]
    final = final_g is not None
    tiles_per_seq = seq // tm
    chunks = tm // RET_CHUNK
    row = lambda w: pl.BlockSpec((tm, w), lambda i: (i, 0))
    col = lambda c: pl.BlockSpec((tm, BRANCH_W), lambda i, c=c: (i, c))
    full = lambda shape: pl.BlockSpec(shape, lambda i: (0,) * len(shape))
    of_layer = lambda shape: pl.BlockSpec((None,) + shape, lambda i: (layer,) + (0,) * len(shape))
    states = pl.BlockSpec((None, HEADS, chunks, 2 * HEAD_D, HEAD_D),
                          lambda i: (i // tiles_per_seq, 0, i % tiles_per_seq, 0, 0))
    in_specs = [row(D_MODEL), full((1, D_MODEL)), row(BRANCH_W), row(BRANCH_W),
                row(BRANCH_W), row(BRANCH_W), col(COL_RV), col(COL_RZ), states,
                full((HEADS, 3, RET_CHUNK, RET_CHUNK)), full((1, BRANCH_W)),
                of_layer((D_MODEL, N_BRANCH * D_MODEL)), of_layer((N_BRANCH, BRANCH_W, D_MODEL)),
                of_layer((D_MODEL, D_MODEL))]
    args = [x, g, ya, yc, qr, kr, proj, proj, st, dec, ret_ng, wg, wbo, wo]
    if final:
        in_specs.append(full((1, D_MODEL)))
        args.append(final_g)
    return pl.pallas_call(
        functools.partial(_merge_body, final),
        grid=(tokens // tm,),
        in_specs=in_specs,
        out_specs=row(D_MODEL),
        out_shape=jax.ShapeDtypeStruct((tokens, D_MODEL), F32),
        scratch_shapes=[pltpu.VMEM((tm, BRANCH_W), BF16)],
        compiler_params=_params("arbitrary"),
        name="merge_final" if final else "merge",
    )(*args)


def _rope_tables(seq):
    inv = ROPE_BASE ** (-jnp.arange(0, HEAD_D, 2, dtype=F32) / HEAD_D)
    ang = jnp.arange(seq, dtype=F32)[:, None] * inv[None, :]
    cos = jnp.cos(ang)
    sin = jnp.sin(ang)
    return jnp.concatenate([cos, cos], axis=-1), jnp.concatenate([-sin, sin], axis=-1)


def _lane_row(p):
    row = jnp.zeros((LANES,), F32)
    row = row.at[0:HEADS].set(p[0]).at[2 * HEADS:3 * HEADS].set(p[1])
    return row[None, :]


def kernel(x, norm_g, w_in, gm_ln_g, gm_ln_b, gm_w_s, gm_b_s, ret_decay_logit, ret_norm_g, gdn_conv_w, gdn_a_log, gdn_dt_bias, gdn_norm_g, w_gate, w_branch_out, w_out, final_norm_g):
    batch, seq, _ = x.shape
    xf = x.reshape(batch * seq, D_MODEL)
    cos, sin = _rope_tables(seq)
    w_main = w_in[:, :, :MAIN_COLS].astype(BF16)
    w_ab = jnp.pad(w_in[:, :, MAIN_COLS:], ((0, 0), (0, 0), (0, LANES - 4 * HEADS))).astype(BF16)
    wg = w_gate.reshape(DEPTH, D_MODEL, N_BRANCH * D_MODEL).astype(BF16)
    wbo = w_branch_out.astype(BF16)
    wo = w_out.astype(BF16)
    for l in range(DEPTH):
        g_row = norm_g[l][None, :]
        bias = jnp.repeat(gm_b_s[l].T, GM_CHUNK, axis=1)
        proj, g_all, gcf, gcb, ya = _in_proj(
            xf, g_row, w_main, w_ab, l, gm_ln_g[l][None, :], gm_ln_b[l][None, :], gm_w_s[l].astype(BF16), bias,
            gdn_conv_w[l], _lane_row(gdn_a_log[l]), _lane_row(gdn_dt_bias[l]), seq)

        dl = jnp.broadcast_to(ret_decay_logit[l][:, :, None, None], (2, HEADS, 8, LANES))
        qr, kr, st, dec = _retention_states(proj, cos, sin, dl)

        yc = _gdn(proj, g_all, gcf, gcb, gdn_norm_g[l][None, :], seq)

        xf = _merge(xf, g_row, ya, yc, proj, qr, kr, st, dec, ret_norm_g[l][None, :], wg, wbo, wo, l, seq,
                    final_norm_g[None, :] if l == DEPTH - 1 else None)
    return xf.reshape(batch, seq, D_MODEL)
```

```python
import functools

import jax
import jax.numpy as jnp
from jax import lax
from jax.experimental import pallas as pl
from jax.experimental.pallas import tpu as pltpu

F32 = jnp.float32
BF16 = jnp.bfloat16

D_MODEL = 1024
DEPTH = 2
BRANCH_W = 512
N_BRANCH = 3
GM_GROUPS = 4
GM_CHUNK = 128
HEADS = 4
HEAD_D = 128
RET_CHUNK = 128
RET_UNROLL = 4
GDN_CHUNK = 64
PHASE_A_CHUNKS = 4
CONV_TILE = 128
ROPE_BASE = 10000.0
EPS = 1e-6
LANES = 128
MAIN_COLS = 11 * BRANCH_W
GM_COLS = 3 * BRANCH_W
RET_COLS = 4 * BRANCH_W
CONV_COLS = 3 * BRANCH_W
PLAIN_COLS = RET_COLS + BRANCH_W
MIX_COLS = PLAIN_COLS + CONV_COLS
COL_RQ, COL_RK, COL_RV, COL_RZ = 0, 1, 2, 3
COL_DZ, COL_DQ, COL_DK, COL_DV = 4, 5, 6, 7
HALO = 8

VMEM_LIMIT = 56 * 1024 * 1024


def _dot(a, b):
    return jnp.dot(a, b, preferred_element_type=F32)


def _dot_nt(a, b):
    return lax.dot_general(a, b, (((1,), (1,)), ((), ())), preferred_element_type=F32)


def _dot_tn(a, b):
    return lax.dot_general(a, b, (((0,), (0,)), ((), ())), preferred_element_type=F32)


def _silu(x):
    return x * jax.nn.sigmoid(x)


def _gelu(x):
    return 0.5 * x * (1.0 + lax.erf(x * 0.7071067811865476))


def _params(*sem):
    return pltpu.CompilerParams(dimension_semantics=sem, vmem_limit_bytes=VMEM_LIMIT)


def _head_slice(h):
    return slice(h * HEAD_D, (h + 1) * HEAD_D)


def _rows(start, size):
    return pl.ds(start if isinstance(start, int) else pl.multiple_of(start, size), size)


def _inproj_body(tiles_per_seq, x_ref, xp_ref, xn_ref, g_ref, w_ref, wab_ref, lng_ref, lnb_ref, ws_ref, bias_ref,
                 cw_ref, alog_ref, dtb_ref, o_ref, gb_ref, gcf_ref, gcb_ref, ya_ref, pc_s):
    tm = x_ref.shape[0]

    def normed(x):
        return (x * lax.rsqrt(jnp.mean(x * x, axis=-1, keepdims=True) + EPS) * g_ref[...]).astype(BF16)

    hb = normed(x_ref[...])
    hb_ext = jnp.concatenate([hb, normed(jnp.concatenate([xp_ref[...], xn_ref[...]], axis=0))], axis=0)
    ab = _dot(hb, wab_ref[...])
    gm = _dot(hb, w_ref[:, 0:GM_COLS])
    o_ref[:, 0:RET_COLS] = _dot(hb, w_ref[:, GM_COLS:GM_COLS + RET_COLS]).astype(o_ref.dtype)
    o_ref[:, RET_COLS:PLAIN_COLS] = _dot(hb, w_ref[:, MAIN_COLS - BRANCH_W:MAIN_COLS]).astype(o_ref.dtype)
    pc = _dot(hb_ext, w_ref[:, GM_COLS + RET_COLS:GM_COLS + RET_COLS + CONV_COLS])

    t_in_seq = pl.program_id(0) % tiles_per_seq
    keep_before = jnp.where(t_in_seq != 0, 1.0, 0.0)
    keep_after = jnp.where(t_in_seq != tiles_per_seq - 1, 1.0, 0.0)
    for s in range(CONV_COLS // LANES):
        cs = slice(s * LANES, (s + 1) * LANES)
        pc_s[s, 0:HALO, :] = pc[tm:tm + HALO, cs] * keep_before
        pc_s[s, HALO:HALO + tm, :] = pc[0:tm, cs]
        pc_s[s, HALO + tm:2 * HALO + tm, :] = pc[tm + HALO:tm + 2 * HALO, cs] * keep_after
    for s in range(CONV_COLS // LANES):
        cs = slice(s * LANES, (s + 1) * LANES)
        w = cw_ref[:, cs]
        piece = s // HEADS
        for j in range(tm // CONV_TILE):
            base = HALO + j * CONV_TILE
            y = _silu(pc_s[s, base - 1:base - 1 + CONV_TILE, :] * w[0:1, :]
                      + pc_s[s, base:base + CONV_TILE, :] * w[1:2, :]
                      + pc_s[s, base + 1:base + 1 + CONV_TILE, :] * w[2:3, :])
            if piece < 2:
                scale = HEAD_D ** -0.5 if piece == 0 else 1.0
                y = y * (lax.rsqrt(jnp.sum(y * y, axis=-1, keepdims=True) + EPS) * scale)
            o_ref[j * CONV_TILE:(j + 1) * CONV_TILE, PLAIN_COLS + s * LANES:PLAIN_COLS + (s + 1) * LANES] = (
                y.astype(o_ref.dtype))

    lane = lax.broadcasted_iota(jnp.int32, (tm, LANES), 1)
    pos = lax.broadcasted_iota(jnp.int32, (tm, LANES), 0) & (GDN_CHUNK - 1)
    g_all = jnp.where((lane & 7) < HEADS,
                      -jnp.exp(alog_ref[...]) * jax.nn.softplus(ab + dtb_ref[...]),
                      jax.nn.sigmoid(ab))
    gb_ref[...] = g_all
    gf = g_all
    gb = g_all
    s = 1
    while s < GDN_CHUNK:
        gf = gf + jnp.where(pos >= s, pltpu.roll(gf, s, 0), 0.0)
        gb = gb + jnp.where(pos < GDN_CHUNK - s, pltpu.roll(gb, tm - s, 0), 0.0)
        s *= 2
    gcf_ref[...] = gf
    gcb_ref[...] = gb

    for c in range(tm // GM_CHUNK):
        r = slice(c * GM_CHUNK, (c + 1) * GM_CHUNK)
        u = _gelu(gm[r, 0:BRANCH_W])
        v = _gelu(gm[r, BRANCH_W:2 * BRANCH_W])
        z = gm[r, 2 * BRANCH_W:3 * BRANCH_W]
        mu = jnp.mean(v, axis=-1, keepdims=True)
        vc = v - mu
        var = jnp.mean(vc * vc, axis=-1, keepdims=True)
        vn = (vc * lax.rsqrt(var + EPS) * lng_ref[...] + lnb_ref[...]).astype(BF16)
        gate = u * _silu(z)
        for g in range(GM_GROUPS):
            cs = slice(g * LANES, (g + 1) * LANES)
            mixed = _dot(ws_ref[g], vn[:, cs]) + bias_ref[:, cs]
            ya_ref[r, cs] = (gate[:, cs] * mixed).astype(ya_ref.dtype)


def _in_proj(x, g, w_main, w_ab, layer, ln_g, ln_b, w_s, bias, conv_w, alog_row, dtb_row, seq, tm=512):
    tokens = x.shape[0]
    assert seq % tm == 0 and tm % CONV_TILE == 0 and tm % GM_CHUNK == 0 and tm % GDN_CHUNK == 0
    halo_blocks = tm // HALO
    row = lambda w: pl.BlockSpec((tm, w), lambda i: (i, 0))
    full = lambda shape, **kw: pl.BlockSpec(shape, lambda i: (0,) * len(shape), **kw)
    before = pl.BlockSpec((HALO, D_MODEL), lambda i: (jnp.maximum(i * halo_blocks - 1, 0), 0))
    after = pl.BlockSpec((HALO, D_MODEL), lambda i: (jnp.minimum((i + 1) * halo_blocks, tokens // HALO - 1), 0))
    return pl.pallas_call(
        functools.partial(_inproj_body, seq // tm),
        grid=(tokens // tm,),
        in_specs=[row(D_MODEL), before, after, full((1, D_MODEL)),
                  pl.BlockSpec((None, D_MODEL, MAIN_COLS), lambda i: (layer, 0, 0), pipeline_mode=pl.Buffered(1)),
                  pl.BlockSpec((None, D_MODEL, LANES), lambda i: (layer, 0, 0)),
                  full((1, BRANCH_W)), full((1, BRANCH_W)), full((GM_GROUPS, GM_CHUNK, GM_CHUNK)),
                  full((GM_CHUNK, BRANCH_W)), full((3, CONV_COLS)), full((1, LANES)), full((1, LANES))],
        out_specs=[row(MIX_COLS), row(LANES), row(LANES), row(LANES), row(BRANCH_W)],
        out_shape=[
            jax.ShapeDtypeStruct((tokens, MIX_COLS), BF16),
            jax.ShapeDtypeStruct((tokens, LANES), F32),
            jax.ShapeDtypeStruct((tokens, LANES), F32),
            jax.ShapeDtypeStruct((tokens, LANES), F32),
            jax.ShapeDtypeStruct((tokens, BRANCH_W), BF16),
        ],
        scratch_shapes=[pltpu.VMEM((CONV_COLS // LANES, tm + 2 * HALO, LANES), F32)],
        compiler_params=_params("arbitrary"),
        name="in_proj",
    )(x, x, x, g, w_main, w_ab, ln_g, ln_b, w_s, bias, conv_w, alog_row, dtb_row)


def _ret_body(q_ref, k_ref, v_ref, cos_ref, sin_ref, dl_ref, qr_ref, kr_ref, st_ref, dec_ref, kv_s):
    C = RET_CHUNK
    n_chunks = q_ref.shape[0] // C
    row = lax.broadcasted_iota(jnp.int32, (C, C), 0).astype(F32)
    colm = lax.broadcasted_iota(jnp.int32, (C, C), 1).astype(F32)
    rel = row - colm

    decs = []
    for h in range(HEADS):
        lf = jnp.broadcast_to(jax.nn.log_sigmoid(dl_ref[0, h])[0:1, :], (C, C))
        lb = jnp.broadcast_to(jax.nn.log_sigmoid(dl_ref[1, h])[0:1, :], (C, C))
        decs.append(dict(kdf=jnp.exp(lf * (C - 1.0 - row)), kdb=jnp.exp(lb * row),
                         cf=jnp.exp(lf * C), cb=jnp.exp(lb * C)))
        dec_ref[h, 0] = jnp.exp(jnp.where(rel >= 0, lf * rel, -lb * rel))
        dec_ref[h, 1] = jnp.exp(lf * (row + 1.0))
        dec_ref[h, 2] = jnp.exp(lb * (C - row))

    def phase1(it, carry):
        cs = [it * RET_UNROLL + t for t in range(RET_UNROLL)]
        rows = [_rows(c * C, C) for c in cs]
        items = [(t, h) for t in range(RET_UNROLL) for h in range(HEADS)]
        cos = [cos_ref[r, :] for r in rows]
        sin = [sin_ref[r, :] for r in rows]
        qs = [q_ref[rows[t], _head_slice(h)].astype(F32) for t, h in items]
        ks = [k_ref[rows[t], _head_slice(h)].astype(F32) for t, h in items]
        qr = [q * cos[t] + pltpu.roll(q, HEAD_D // 2, 1) * sin[t] for (t, h), q in zip(items, qs)]
        kr = [(k * cos[t] + pltpu.roll(k, HEAD_D // 2, 1) * sin[t]) * (HEAD_D ** -0.5) for (t, h), k in zip(items, ks)]
        kcat = [jnp.concatenate([k * decs[h]["kdf"], k * decs[h]["kdb"]], axis=1).astype(BF16)
                for (t, h), k in zip(items, kr)]
        kvs = [_dot_tn(kc, v_ref[rows[t], _head_slice(h)]) for (t, h), kc in zip(items, kcat)]
        for n, (t, h) in enumerate(items):
            qr_ref[rows[t], _head_slice(h)] = qr[n].astype(qr_ref.dtype)
            kr_ref[rows[t], _head_slice(h)] = kr[n].astype(kr_ref.dtype)
            kv_s[h, cs[t]] = kvs[n]
        return carry

    lax.fori_loop(0, n_chunks // RET_UNROLL, phase1, 0)

    for h in range(HEADS):
        cf, cb = decs[h]["cf"], decs[h]["cb"]

        def scan(i, carry, h=h, cf=cf, cb=cb):
            sf, sb = carry
            j = n_chunks - 1 - i
            st_ref[h, i, 0:HEAD_D, :] = sf.astype(st_ref.dtype)
            st_ref[h, j, HEAD_D:2 * HEAD_D, :] = sb.astype(st_ref.dtype)
            return sf * cf + kv_s[h, i, 0:HEAD_D, :], sb * cb + kv_s[h, j, HEAD_D:2 * HEAD_D, :]

        zero = jnp.zeros((HEAD_D, HEAD_D), F32)
        lax.fori_loop(0, n_chunks, scan, (zero, zero))


def _retention_states(proj, cos, sin, dl):
    seq = cos.shape[0]
    batch = proj.shape[0] // seq
    n_chunks = seq // RET_CHUNK
    assert seq % (RET_CHUNK * RET_UNROLL) == 0
    col = lambda c: pl.BlockSpec((seq, BRANCH_W), lambda b, c=c: (b, c))
    full = lambda shape: pl.BlockSpec(shape, lambda b: (0,) * len(shape))
    rows = pl.BlockSpec((seq, BRANCH_W), lambda b: (b, 0))
    return pl.pallas_call(
        _ret_body,
        grid=(batch,),
        in_specs=[col(COL_RQ), col(COL_RK), col(COL_RV),
                  full((seq, HEAD_D)), full((seq, HEAD_D)), full((2, HEADS, 8, LANES))],
        out_specs=[rows, rows,
                   pl.BlockSpec((None, HEADS, n_chunks, 2 * HEAD_D, HEAD_D), lambda b: (b, 0, 0, 0, 0)),
                   full((HEADS, 3, RET_CHUNK, RET_CHUNK))],
        out_shape=[jax.ShapeDtypeStruct((batch * seq, BRANCH_W), BF16),
                   jax.ShapeDtypeStruct((batch * seq, BRANCH_W), BF16),
                   jax.ShapeDtypeStruct((batch, HEADS, n_chunks, 2 * HEAD_D, HEAD_D), BF16),
                   jax.ShapeDtypeStruct((HEADS, 3, RET_CHUNK, RET_CHUNK), F32)],
        scratch_shapes=[pltpu.VMEM((HEADS, n_chunks, 2 * HEAD_D, HEAD_D), F32)],
        compiler_params=_params("arbitrary"),
        name="retention",
    )(proj, proj, proj, cos, sin, dl)


def _gdn_body(qn_s, kn_s, vn_s, z_ref, g_s, gcf_s, gcb_s, ng_ref, y_ref, u_s, w_s, sc_s, tri_s, o_s, st_s):
    C = GDN_CHUNK
    seq = qn_s.shape[0]
    n_chunks = seq // C

    ri = lax.broadcasted_iota(jnp.int32, (C, C), 0)
    ci = lax.broadcasted_iota(jnp.int32, (C, C), 1)
    eye = (ri == ci).astype(F32)
    masks = ((ri >= ci, ri > ci), (ri <= ci, ri < ci))
    levels = [((ri >> (j + 1)) == (ci >> (j + 1))) & ((ri >> j) != (ci >> j)) for j in range(6)]
    chains = [(d, h) for d in range(2) for h in range(HEADS)]
    gc_refs = (gcf_s, gcb_s)

    U = PHASE_A_CHUNKS
    n_groups = n_chunks // U

    work = [(t, d, h) for t in range(U) for d, h in chains]

    def group_rows(grp):
        return ([_rows((grp * U + t) * C, C) for t in range(U)],
                [_rows((n_chunks - 1 - grp * U - t) * C, C) for t in range(U)])

    def prep(grp):
        rows = group_rows(grp)
        gcs = {(t, d): gc_refs[d][rows[d][t], :] for t in range(U) for d in range(2)}
        gts = {key: g.T for key, g in gcs.items()}
        gbeta = {(t, d): g_s[rows[d][t], :] for t in range(U) for d in range(2)}
        for t, d, h in work:
            r, hs = rows[d][t], _head_slice(h)
            incl, strict = masks[d]
            gl, bl = 8 * d + h, 8 * d + HEADS + h
            k16 = kn_s[r, hs]
            kk = _dot_nt(jnp.concatenate([k16, qn_s[r, hs]], axis=0), k16)
            gcol = gcs[t, d][:, gl:gl + 1]
            beta = gbeta[t, d][:, bl:bl + 1]
            grow = jnp.broadcast_to(gts[t, d][gl:gl + 1, :], (C, C))
            decay = jnp.exp(jnp.where(incl, gcol - grow, -jnp.inf))
            tri_s[d, h, r, :] = jnp.where(strict, beta * kk[:C] * decay, 0.0).astype(tri_s.dtype)
            sc_s[d, h, r, :] = (kk[C:] * decay).astype(sc_s.dtype)

    def solve(grp):
        rows = group_rows(grp)
        zero16 = jnp.zeros((C, C), BF16)
        tri = [tri_s[d, h, rows[d][t], :] for t, d, h in work]
        inv = [eye - jnp.where(levels[0], x, zero16).astype(F32) for x in tri]
        for j in range(1, 6):
            m16 = [jnp.where(levels[j], x, zero16) for x in tri]
            i16 = [p.astype(BF16) for p in inv]
            mi = [_dot(m, p).astype(BF16) for m, p in zip(m16, i16)]
            inv = [p - _dot(p16, x) for p, p16, x in zip(inv, i16, mi)]
        gcs = {(t, d): gc_refs[d][rows[d][t], :] for t in range(U) for d in range(2)}
        gbeta = {(t, d): g_s[rows[d][t], :] for t in range(U) for d in range(2)}
        for (t, d, h), p in zip(work, inv):
            r, hs = rows[d][t], _head_slice(h)
            gl, bl = 8 * d + h, 8 * d + HEADS + h
            gcol = jnp.broadcast_to(gcs[t, d][:, gl:gl + 1], (C, HEAD_D))
            beta = jnp.broadcast_to(gbeta[t, d][:, bl:bl + 1], (C, HEAD_D))
            kbeta = kn_s[r, hs].astype(F32) * beta
            rhs = jnp.concatenate([vn_s[r, hs].astype(F32) * beta, kbeta * jnp.exp(gcol)], axis=1).astype(BF16)
            sol = _dot(p.astype(BF16), rhs)
            u_s[d, r, hs] = sol[:, :HEAD_D].astype(u_s.dtype)
            w_s[d, r, hs] = sol[:, HEAD_D:].astype(w_s.dtype)

    def recur(grp, final):
        rows = group_rows(grp)
        st = [st_s[n] for n in range(len(chains))]
        for t in range(U):
            gcs = [gc_refs[d][rows[d][t], :] for d in range(2)]
            lhs, kd, egl = [], [], []
            for d, h in chains:
                r, hs = rows[d][t], _head_slice(h)
                gl = 8 * d + h
                gcol = jnp.broadcast_to(gcs[d][:, gl:gl + 1], (C, HEAD_D))
                last = C - 1 if d == 0 else 0
                glast = gcol[last:last + 1, :]
                qe = (qn_s[r, hs].astype(F32) * jnp.exp(gcol)).astype(BF16)
                lhs.append(jnp.concatenate([w_s[d, r, hs], qe], axis=0))
                kd.append((kn_s[r, hs].astype(F32) * jnp.exp(glast - gcol)).astype(BF16))
                egl.append(jnp.exp(glast))
            ws = [_dot(a, s_.astype(BF16)) for a, s_ in zip(lhs, st)]
            vnew = [(u_s[d, rows[d][t], _head_slice(h)].astype(F32) - x[:C]).astype(BF16)
                    for (d, h), x in zip(chains, ws)]
            outs = [ws[n][C:] + _dot(sc_s[d, h, rows[d][t], :], vnew[n]) for n, (d, h) in enumerate(chains)]
            for n, (d, h) in enumerate(chains):
                r, hs = rows[d][t], _head_slice(h)
                if final:
                    o = outs[n] + o_s[r, hs].astype(F32)
                    on = o * lax.rsqrt(jnp.mean(o * o, axis=-1, keepdims=True) + EPS) * ng_ref[...]
                    y_ref[r, hs] = (on * _silu(z_ref[r, hs].astype(F32))).astype(y_ref.dtype)
                else:
                    o_s[r, hs] = outs[n].astype(o_s.dtype)
            st = [s_ * e + _dot_tn(a, x) for s_, e, a, x in zip(st, egl, kd, vnew)]
        for n, s_ in enumerate(st):
            st_s[n] = s_

    half = n_groups // 2
    st_s[...] = jnp.zeros_like(st_s)
    prep(0)
    solve(0)
    prep(1)

    def pipelined(final, grp, carry):
        recur(grp - 2, final)
        solve(grp - 1)
        prep(grp)
        return carry

    lax.fori_loop(2, half + 2, functools.partial(pipelined, False), 0)
    lax.fori_loop(half + 2, n_groups, functools.partial(pipelined, True), 0)
    recur(n_groups - 2, True)
    solve(n_groups - 1)
    recur(n_groups - 1, True)


def _gdn(proj, g_all, gcf, gcb, ng, seq):
    batch = proj.shape[0] // seq
    n_groups = seq // (GDN_CHUNK * PHASE_A_CHUNKS)
    assert seq % (GDN_CHUNK * PHASE_A_CHUNKS) == 0 and n_groups >= 4 and n_groups % 2 == 0, (
        "the 3-stage group pipeline needs an even number (>= 4) of chunk groups per sequence")
    col = lambda c: pl.BlockSpec((seq, BRANCH_W), lambda b, c=c: (b, c))
    lanes = pl.BlockSpec((seq, LANES), lambda b: (b, 0))
    full = lambda shape: pl.BlockSpec(shape, lambda b: (0,) * len(shape))
    per_chain = pltpu.VMEM((2, HEADS, seq, GDN_CHUNK), BF16)
    return pl.pallas_call(
        _gdn_body,
        grid=(batch,),
        in_specs=[col(COL_DQ), col(COL_DK), col(COL_DV), col(COL_DZ), lanes, lanes, lanes, full((1, HEAD_D))],
        out_specs=pl.BlockSpec((seq, BRANCH_W), lambda b: (b, 0)),
        out_shape=jax.ShapeDtypeStruct((batch * seq, BRANCH_W), BF16),
        scratch_shapes=[pltpu.VMEM((2, seq, BRANCH_W), BF16), pltpu.VMEM((2, seq, BRANCH_W), BF16),
                        per_chain, per_chain, pltpu.VMEM((seq, BRANCH_W), BF16),
                        pltpu.VMEM((2 * HEADS, HEAD_D, HEAD_D), F32)],
        compiler_params=_params("arbitrary"),
        name="gdn",
    )(proj, proj, proj, proj, g_all, gcf, gcb, ng)


def _merge_body(final, x_ref, g_ref, ya_ref, yc_ref, qr_ref, kr_ref, rv_ref, rz_ref, st_ref, dec_ref, rng_ref,
                wg_ref, wbo_ref, wo_ref, *rest):
    if final:
        fg_ref, o_ref, yb_s = rest
    else:
        o_ref, yb_s = rest
    C = RET_CHUNK
    items = [(t, h) for t in range(x_ref.shape[0] // C) for h in range(HEADS)]
    rows = lambda t: slice(t * C, (t + 1) * C)
    qr = [qr_ref[rows(t), _head_slice(h)] for t, h in items]
    sc = [_dot_nt(q, kr_ref[rows(t), _head_slice(h)]) * dec_ref[h, 0] for (t, h), q in zip(items, qr)]
    lhs, rhs = [], []
    for n, (t, h) in enumerate(items):
        q = qr[n].astype(F32)
        lhs.append(jnp.concatenate([sc[n], q * dec_ref[h, 1], q * dec_ref[h, 2]], axis=1).astype(BF16))
        rhs.append(jnp.concatenate([rv_ref[rows(t), _head_slice(h)], st_ref[h, t]], axis=0))
    outs = [_dot(a, b) for a, b in zip(lhs, rhs)]
    mus = [jnp.mean(o, axis=-1, keepdims=True) for o in outs]
    ocs = [o - mu for o, mu in zip(outs, mus)]
    var = [jnp.mean(oc * oc, axis=-1, keepdims=True) for oc in ocs]
    for n, (t, h) in enumerate(items):
        hs = _head_slice(h)
        on = ocs[n] * lax.rsqrt(var[n] + EPS) * rng_ref[:, hs]
        yb_s[rows(t), hs] = (on * _silu(rz_ref[rows(t), hs].astype(F32))).astype(yb_s.dtype)

    x = x_ref[...]
    h = (x * lax.rsqrt(jnp.mean(x * x, axis=-1, keepdims=True) + EPS) * g_ref[...]).astype(BF16)
    merged = None
    for n, y_ref in enumerate((ya_ref, yb_s, yc_ref)):
        gate = jax.nn.sigmoid(_dot(h, wg_ref[n]))
        term = gate * _dot(y_ref[...], wbo_ref[n])
        merged = term if merged is None else merged + term
    out = x + _dot(merged.astype(BF16), wo_ref[...])
    if final:
        out = out * lax.rsqrt(jnp.mean(out * out, axis=-1, keepdims=True) + EPS) * fg_ref[...]
    o_ref[...] = out


def _merge(x, g, ya, yc, proj, qr, kr, st, dec, ret_ng, wg, wbo, wo, layer, seq, final_g=None, tm=512):
    tokens = x.shape[0]
    final = final_g is not None
    assert seq % tm == 0 and tm % RET_CHUNK == 0
    tiles_per_seq = seq // tm
    chunks = tm // RET_CHUNK
    row = lambda w: pl.BlockSpec((tm, w), lambda i: (i, 0))
    col = lambda c: pl.BlockSpec((tm, BRANCH_W), lambda i, c=c: (i, c))
    full = lambda shape: pl.BlockSpec(shape, lambda i: (0,) * len(shape))
    of_layer = lambda shape: pl.BlockSpec((None,) + shape, lambda i: (layer,) + (0,) * len(shape))
    states = pl.BlockSpec((None, HEADS, chunks, 2 * HEAD_D, HEAD_D),
                          lambda i: (i // tiles_per_seq, 0, i % tiles_per_seq, 0, 0))
    in_specs = [row(D_MODEL), full((1, D_MODEL)), row(BRANCH_W), row(BRANCH_W),
                row(BRANCH_W), row(BRANCH_W), col(COL_RV), col(COL_RZ), states,
                full((HEADS, 3, RET_CHUNK, RET_CHUNK)), full((1, BRANCH_W)),
                of_layer((N_BRANCH, D_MODEL, D_MODEL)), of_layer((N_BRANCH, BRANCH_W, D_MODEL)),
                of_layer((D_MODEL, D_MODEL))]
    args = [x, g, ya, yc, qr, kr, proj, proj, st, dec, ret_ng, wg, wbo, wo]
    if final:
        in_specs.append(full((1, D_MODEL)))
        args.append(final_g)
    return pl.pallas_call(
        functools.partial(_merge_body, final),
        grid=(tokens // tm,),
        in_specs=in_specs,
        out_specs=row(D_MODEL),
        out_shape=jax.ShapeDtypeStruct((tokens, D_MODEL), F32),
        scratch_shapes=[pltpu.VMEM((tm, BRANCH_W), BF16)],
        compiler_params=_params("arbitrary"),
        name="merge_final" if final else "merge",
    )(*args)


def _rope_tables(seq):
    inv = ROPE_BASE ** (-jnp.arange(0, HEAD_D, 2, dtype=F32) / HEAD_D)
    ang = jnp.arange(seq, dtype=F32)[:, None] * inv[None, :]
    cos = jnp.cos(ang)
    sin = jnp.sin(ang)
    return jnp.concatenate([cos, cos], axis=-1), jnp.concatenate([-sin, sin], axis=-1)


def _lane_row(p):
    row = jnp.zeros((LANES,), F32)
    row = row.at[0:HEADS].set(p[0]).at[2 * HEADS:3 * HEADS].set(p[1])
    return row[None, :]


def kernel(x, norm_g, w_in, gm_ln_g, gm_ln_b, gm_w_s, gm_b_s, ret_decay_logit, ret_norm_g, gdn_conv_w, gdn_a_log, gdn_dt_bias, gdn_norm_g, w_gate, w_branch_out, w_out, final_norm_g):
    batch, seq, _ = x.shape
    xf = x.reshape(batch * seq, D_MODEL)
    cos, sin = _rope_tables(seq)
    w_main = w_in.astype(BF16)
    w_ab = jnp.pad(w_main[:, :, MAIN_COLS:], ((0, 0), (0, 0), (0, LANES - 4 * HEADS)))
    wg = jnp.transpose(w_gate, (0, 2, 1, 3)).astype(BF16)
    wbo = w_branch_out.astype(BF16)
    wo = w_out.astype(BF16)
    for l in range(DEPTH):
        g_row = norm_g[l][None, :]
        bias = jnp.repeat(gm_b_s[l].T, GM_CHUNK, axis=1)
        proj, g_all, gcf, gcb, ya = _in_proj(
            xf, g_row, w_main, w_ab, l, gm_ln_g[l][None, :], gm_ln_b[l][None, :], gm_w_s[l].astype(BF16), bias,
            gdn_conv_w[l], _lane_row(gdn_a_log[l]), _lane_row(gdn_dt_bias[l]), seq)

        dl = jnp.broadcast_to(ret_decay_logit[l][:, :, None, None], (2, HEADS, 8, LANES))
        qr, kr, st, dec = _retention_states(proj, cos, sin, dl)

        yc = _gdn(proj, g_all, gcf, gcb, gdn_norm_g[l][None, :], seq)

        xf = _merge(xf, g_row, ya, yc, proj, qr, kr, st, dec, ret_norm_g[l][None, :], wg, wbo, wo, l, seq,
                    final_norm_g[None, :] if l == DEPTH - 1 else None)
    return xf.reshape(batch, seq, D_MODEL)
```

```python
import functools

import jax
import jax.numpy as jnp
from jax import lax
from jax.experimental import pallas as pl
from jax.experimental.pallas import tpu as pltpu

F32 = jnp.float32
BF16 = jnp.bfloat16

D_MODEL = 1024
DEPTH = 2
BRANCH_W = 512
N_BRANCH = 3
GM_GROUPS = 4
GM_CHUNK = 128
HEADS = 4
HEAD_D = 128
RET_CHUNK = 128
RET_UNROLL = 4
GDN_CHUNK = 64
PHASE_A_CHUNKS = 4
CONV_TILE = 128
ROPE_BASE = 10000.0
EPS = 1e-6
LANES = 128
MAIN_COLS = 11 * BRANCH_W
GM_COLS = 3 * BRANCH_W
RET_COLS = 4 * BRANCH_W
CONV_COLS = 3 * BRANCH_W
PLAIN_COLS = RET_COLS + BRANCH_W
MIX_COLS = PLAIN_COLS + CONV_COLS
COL_RQ, COL_RK, COL_RV, COL_RZ = 0, 1, 2, 3
COL_DZ, COL_DQ, COL_DK, COL_DV = 4, 5, 6, 7
HALO = 8

VMEM_LIMIT = 56 * 1024 * 1024


def _dot(a, b):
    return jnp.dot(a, b, preferred_element_type=F32)


def _dot_nt(a, b):
    return lax.dot_general(a, b, (((1,), (1,)), ((), ())), preferred_element_type=F32)


def _dot_tn(a, b):
    return lax.dot_general(a, b, (((0,), (0,)), ((), ())), preferred_element_type=F32)


def _silu(x):
    return x * jax.nn.sigmoid(x)


def _gelu(x):
    return 0.5 * x * (1.0 + lax.erf(x * 0.7071067811865476))


def _params(*sem):
    return pltpu.CompilerParams(dimension_semantics=sem, vmem_limit_bytes=VMEM_LIMIT)


def _head_slice(h):
    return slice(h * HEAD_D, (h + 1) * HEAD_D)


def _rows(start, size):
    return pl.ds(start if isinstance(start, int) else pl.multiple_of(start, size), size)


def _inproj_body(tiles_per_seq, x_ref, xp_ref, xn_ref, g_ref, w_ref, wab_ref, lng_ref, lnb_ref, ws_ref, bias_ref,
                 cw_ref, alog_ref, dtb_ref, o_ref, gb_ref, gcf_ref, gcb_ref, ya_ref, pc_s):
    tm = x_ref.shape[0]

    def normed(x):
        return (x * lax.rsqrt(jnp.mean(x * x, axis=-1, keepdims=True) + EPS) * g_ref[...]).astype(BF16)

    hb = normed(x_ref[...])
    hb_ext = jnp.concatenate([hb, normed(jnp.concatenate([xp_ref[...], xn_ref[...]], axis=0))], axis=0)
    ab = _dot(hb, wab_ref[...])
    gm = _dot(hb, w_ref[:, 0:GM_COLS])
    o_ref[:, 0:RET_COLS] = _dot(hb, w_ref[:, GM_COLS:GM_COLS + RET_COLS]).astype(o_ref.dtype)
    o_ref[:, RET_COLS:PLAIN_COLS] = _dot(hb, w_ref[:, MAIN_COLS - BRANCH_W:MAIN_COLS]).astype(o_ref.dtype)
    pc = _dot(hb_ext, w_ref[:, GM_COLS + RET_COLS:GM_COLS + RET_COLS + CONV_COLS])

    t_in_seq = pl.program_id(0) % tiles_per_seq
    keep_before = jnp.where(t_in_seq != 0, 1.0, 0.0)
    keep_after = jnp.where(t_in_seq != tiles_per_seq - 1, 1.0, 0.0)
    for s in range(CONV_COLS // LANES):
        cs = slice(s * LANES, (s + 1) * LANES)
        pc_s[s, 0:HALO, :] = pc[tm:tm + HALO, cs] * keep_before
        pc_s[s, HALO:HALO + tm, :] = pc[0:tm, cs]
        pc_s[s, HALO + tm:2 * HALO + tm, :] = pc[tm + HALO:tm + 2 * HALO, cs] * keep_after
    for s in range(CONV_COLS // LANES):
        cs = slice(s * LANES, (s + 1) * LANES)
        w = cw_ref[:, cs]
        piece = s // HEADS
        for j in range(tm // CONV_TILE):
            base = HALO + j * CONV_TILE
            y = _silu(pc_s[s, base - 1:base - 1 + CONV_TILE, :] * w[0:1, :]
                      + pc_s[s, base:base + CONV_TILE, :] * w[1:2, :]
                      + pc_s[s, base + 1:base + 1 + CONV_TILE, :] * w[2:3, :])
            if piece < 2:
                scale = HEAD_D ** -0.5 if piece == 0 else 1.0
                y = y * (lax.rsqrt(jnp.sum(y * y, axis=-1, keepdims=True) + EPS) * scale)
            o_ref[j * CONV_TILE:(j + 1) * CONV_TILE, PLAIN_COLS + s * LANES:PLAIN_COLS + (s + 1) * LANES] = (
                y.astype(o_ref.dtype))

    lane = lax.broadcasted_iota(jnp.int32, (tm, LANES), 1)
    pos = lax.broadcasted_iota(jnp.int32, (tm, LANES), 0) & (GDN_CHUNK - 1)
    g_all = jnp.where((lane & 7) < HEADS,
                      -jnp.exp(alog_ref[...]) * jax.nn.softplus(ab + dtb_ref[...]),
                      jax.nn.sigmoid(ab))
    gb_ref[...] = g_all
    gf = g_all
    gb = g_all
    s = 1
    while s < GDN_CHUNK:
        gf = gf + jnp.where(pos >= s, pltpu.roll(gf, s, 0), 0.0)
        gb = gb + jnp.where(pos < GDN_CHUNK - s, pltpu.roll(gb, tm - s, 0), 0.0)
        s *= 2
    gcf_ref[...] = gf
    gcb_ref[...] = gb

    for c in range(tm // GM_CHUNK):
        r = slice(c * GM_CHUNK, (c + 1) * GM_CHUNK)
        u = _gelu(gm[r, 0:BRANCH_W])
        v = _gelu(gm[r, BRANCH_W:2 * BRANCH_W])
        z = gm[r, 2 * BRANCH_W:3 * BRANCH_W]
        mu = jnp.mean(v, axis=-1, keepdims=True)
        vc = v - mu
        var = jnp.mean(vc * vc, axis=-1, keepdims=True)
        vn = (vc * lax.rsqrt(var + EPS) * lng_ref[...] + lnb_ref[...]).astype(BF16)
        gate = u * _silu(z)
        for g in range(GM_GROUPS):
            cs = slice(g * LANES, (g + 1) * LANES)
            mixed = _dot(ws_ref[g], vn[:, cs]) + bias_ref[:, cs]
            ya_ref[r, cs] = (gate[:, cs] * mixed).astype(ya_ref.dtype)


def _in_proj(x, g, w_main, w_ab, layer, ln_g, ln_b, w_s, bias, conv_w, alog_row, dtb_row, seq, tm=512):
    tokens = x.shape[0]
    assert seq % tm == 0 and tm % CONV_TILE == 0 and tm % GM_CHUNK == 0 and tm % GDN_CHUNK == 0
    halo_blocks = tm // HALO
    row = lambda w: pl.BlockSpec((tm, w), lambda i: (i, 0))
    full = lambda shape, **kw: pl.BlockSpec(shape, lambda i: (0,) * len(shape), **kw)
    before = pl.BlockSpec((HALO, D_MODEL), lambda i: (jnp.maximum(i * halo_blocks - 1, 0), 0))
    after = pl.BlockSpec((HALO, D_MODEL), lambda i: (jnp.minimum((i + 1) * halo_blocks, tokens // HALO - 1), 0))
    return pl.pallas_call(
        functools.partial(_inproj_body, seq // tm),
        grid=(tokens // tm,),
        in_specs=[row(D_MODEL), before, after, full((1, D_MODEL)),
                  pl.BlockSpec((None, D_MODEL, MAIN_COLS), lambda i: (layer, 0, 0), pipeline_mode=pl.Buffered(1)),
                  pl.BlockSpec((None, D_MODEL, LANES), lambda i: (layer, 0, 0)),
                  full((1, BRANCH_W)), full((1, BRANCH_W)), full((GM_GROUPS, GM_CHUNK, GM_CHUNK)),
                  full((GM_CHUNK, BRANCH_W)), full((3, CONV_COLS)), full((1, LANES)), full((1, LANES))],
        out_specs=[row(MIX_COLS), row(LANES), row(LANES), row(LANES), row(BRANCH_W)],
        out_shape=[
            jax.ShapeDtypeStruct((tokens, MIX_COLS), BF16),
            jax.ShapeDtypeStruct((tokens, LANES), F32),
            jax.ShapeDtypeStruct((tokens, LANES), F32),
            jax.ShapeDtypeStruct((tokens, LANES), F32),
            jax.ShapeDtypeStruct((tokens, BRANCH_W), BF16),
        ],
        scratch_shapes=[pltpu.VMEM((CONV_COLS // LANES, tm + 2 * HALO, LANES), F32)],
        compiler_params=_params("arbitrary"),
        name="in_proj",
    )(x, x, x, g, w_main, w_ab, ln_g, ln_b, w_s, bias, conv_w, alog_row, dtb_row)


def _ret_body(q_ref, k_ref, v_ref, cos_ref, sin_ref, dl_ref, qr_ref, kr_ref, st_ref, dec_ref, kv_s):
    C = RET_CHUNK
    n_chunks = q_ref.shape[0] // C
    row = lax.broadcasted_iota(jnp.int32, (C, C), 0).astype(F32)
    colm = lax.broadcasted_iota(jnp.int32, (C, C), 1).astype(F32)
    rel = row - colm

    decs = []
    for h in range(HEADS):
        lf = jnp.broadcast_to(jax.nn.log_sigmoid(dl_ref[0, h])[0:1, :], (C, C))
        lb = jnp.broadcast_to(jax.nn.log_sigmoid(dl_ref[1, h])[0:1, :], (C, C))
        decs.append(dict(kdf=jnp.exp(lf * (C - 1.0 - row)), kdb=jnp.exp(lb * row),
                         cf=jnp.exp(lf * C), cb=jnp.exp(lb * C)))
        dec_ref[h, 0] = jnp.exp(jnp.where(rel >= 0, lf * rel, -lb * rel))
        dec_ref[h, 1] = jnp.exp(lf * (row + 1.0))
        dec_ref[h, 2] = jnp.exp(lb * (C - row))

    def phase1(it, carry):
        cs = [it * RET_UNROLL + t for t in range(RET_UNROLL)]
        rows = [_rows(c * C, C) for c in cs]
        items = [(t, h) for t in range(RET_UNROLL) for h in range(HEADS)]
        cos = [cos_ref[r, :] for r in rows]
        sin = [sin_ref[r, :] for r in rows]
        qs = [q_ref[rows[t], _head_slice(h)].astype(F32) for t, h in items]
        ks = [k_ref[rows[t], _head_slice(h)].astype(F32) for t, h in items]
        qr = [q * cos[t] + pltpu.roll(q, HEAD_D // 2, 1) * sin[t] for (t, h), q in zip(items, qs)]
        kr = [(k * cos[t] + pltpu.roll(k, HEAD_D // 2, 1) * sin[t]) * (HEAD_D ** -0.5) for (t, h), k in zip(items, ks)]
        kcat = [jnp.concatenate([k * decs[h]["kdf"], k * decs[h]["kdb"]], axis=1).astype(BF16)
                for (t, h), k in zip(items, kr)]
        kvs = [_dot_tn(kc, v_ref[rows[t], _head_slice(h)]) for (t, h), kc in zip(items, kcat)]
        for n, (t, h) in enumerate(items):
            qr_ref[rows[t], _head_slice(h)] = qr[n].astype(qr_ref.dtype)
            kr_ref[rows[t], _head_slice(h)] = kr[n].astype(kr_ref.dtype)
            kv_s[h, cs[t]] = kvs[n]
        return carry

    lax.fori_loop(0, n_chunks // RET_UNROLL, phase1, 0)

    for h in range(HEADS):
        cf, cb = decs[h]["cf"], decs[h]["cb"]

        def scan(i, carry, h=h, cf=cf, cb=cb):
            sf, sb = carry
            j = n_chunks - 1 - i
            st_ref[h, i, 0:HEAD_D, :] = sf.astype(st_ref.dtype)
            st_ref[h, j, HEAD_D:2 * HEAD_D, :] = sb.astype(st_ref.dtype)
            return sf * cf + kv_s[h, i, 0:HEAD_D, :], sb * cb + kv_s[h, j, HEAD_D:2 * HEAD_D, :]

        zero = jnp.zeros((HEAD_D, HEAD_D), F32)
        lax.fori_loop(0, n_chunks, scan, (zero, zero))


def _retention_states(proj, cos, sin, dl):
    seq = cos.shape[0]
    batch = proj.shape[0] // seq
    n_chunks = seq // RET_CHUNK
    assert seq % (RET_CHUNK * RET_UNROLL) == 0
    col = lambda c: pl.BlockSpec((seq, BRANCH_W), lambda b, c=c: (b, c))
    full = lambda shape: pl.BlockSpec(shape, lambda b: (0,) * len(shape))
    rows = pl.BlockSpec((seq, BRANCH_W), lambda b: (b, 0))
    return pl.pallas_call(
        _ret_body,
        grid=(batch,),
        in_specs=[col(COL_RQ), col(COL_RK), col(COL_RV),
                  full((seq, HEAD_D)), full((seq, HEAD_D)), full((2, HEADS, 8, LANES))],
        out_specs=[rows, rows,
                   pl.BlockSpec((None, HEADS, n_chunks, 2 * HEAD_D, HEAD_D), lambda b: (b, 0, 0, 0, 0)),
                   full((HEADS, 3, RET_CHUNK, RET_CHUNK))],
        out_shape=[jax.ShapeDtypeStruct((batch * seq, BRANCH_W), BF16),
                   jax.ShapeDtypeStruct((batch * seq, BRANCH_W), BF16),
                   jax.ShapeDtypeStruct((batch, HEADS, n_chunks, 2 * HEAD_D, HEAD_D), BF16),
                   jax.ShapeDtypeStruct((HEADS, 3, RET_CHUNK, RET_CHUNK), F32)],
        scratch_shapes=[pltpu.VMEM((HEADS, n_chunks, 2 * HEAD_D, HEAD_D), F32)],
        compiler_params=_params("arbitrary"),
        name="retention",
    )(proj, proj, proj, cos, sin, dl)


def _gdn_body(qn_s, kn_s, vn_s, z_ref, g_s, gcf_s, gcb_s, ng_ref, y_ref, u_s, w_s, sc_s, tri_s, o_s, st_s):
    C = GDN_CHUNK
    seq = qn_s.shape[0]
    n_chunks = seq // C

    ri = lax.broadcasted_iota(jnp.int32, (C, C), 0)
    ci = lax.broadcasted_iota(jnp.int32, (C, C), 1)
    masks = ((ri >= ci, ri > ci), (ri <= ci, ri < ci))
    rp = lax.broadcasted_iota(jnp.int32, (C, 2 * C), 0)
    lp = lax.broadcasted_iota(jnp.int32, (C, 2 * C), 1)
    cp = lp & (C - 1)
    left = lp < C
    eye = (rp == cp).astype(F32)
    levels = [((rp >> (j + 1)) == (cp >> (j + 1))) & ((rp >> j) != (cp >> j)) for j in range(6)]

    def blockdiag(xp):
        zero = jnp.zeros_like(xp)
        return jnp.concatenate([jnp.where(left, xp, zero), jnp.where(left, zero, xp)], axis=0)
    chains = [(d, h) for d in range(2) for h in range(HEADS)]
    gc_refs = (gcf_s, gcb_s)

    U = PHASE_A_CHUNKS
    n_groups = n_chunks // U

    work = [(t, d, h) for t in range(U) for d, h in chains]
    pairs = [(t, d, p) for t in range(U) for d in range(2) for p in range(HEADS // 2)]

    def group_rows(grp):
        return ([_rows((grp * U + t) * C, C) for t in range(U)],
                [_rows((n_chunks - 1 - grp * U - t) * C, C) for t in range(U)])

    def prep(grp):
        rows = group_rows(grp)
        gcs = {(t, d): gc_refs[d][rows[d][t], :] for t in range(U) for d in range(2)}
        gts = {key: g.T for key, g in gcs.items()}
        gbeta = {(t, d): g_s[rows[d][t], :] for t in range(U) for d in range(2)}
        for t, d, h in work:
            r, hs = rows[d][t], _head_slice(h)
            incl, strict = masks[d]
            gl, bl = 8 * d + h, 8 * d + HEADS + h
            k16 = kn_s[r, hs]
            kk = _dot_nt(jnp.concatenate([k16, qn_s[r, hs]], axis=0), k16)
            gcol = gcs[t, d][:, gl:gl + 1]
            beta = gbeta[t, d][:, bl:bl + 1]
            grow = jnp.broadcast_to(gts[t, d][gl:gl + 1, :], (C, C))
            decay = jnp.exp(jnp.where(incl, gcol - grow, -jnp.inf))
            tri = jnp.where(strict, beta * kk[:C] * decay, 0.0)
            if h % 2 == 0:
                tri_even = tri
            else:
                tri_s[d, h // 2, r, :] = jnp.concatenate([tri_even, tri], axis=1).astype(tri_s.dtype)
            sc_s[d, h, r, :] = (kk[C:] * decay).astype(sc_s.dtype)

    def solve(grp):
        rows = group_rows(grp)
        zero16 = jnp.zeros((C, 2 * C), BF16)
        tri = [tri_s[d, p, rows[d][t], :] for t, d, p in pairs]
        inv = [eye - jnp.where(levels[0], x, zero16).astype(F32) for x in tri]
        for j in range(1, 6):
            m16 = [jnp.where(levels[j], x, zero16) for x in tri]
            i16 = [q.astype(BF16) for q in inv]
            mi = [_dot(m, blockdiag(q)).astype(BF16) for m, q in zip(m16, i16)]
            yield
            inv = [q - _dot(q16, blockdiag(x)) for q, q16, x in zip(inv, i16, mi)]
            yield
        gcs = {(t, d): gc_refs[d][rows[d][t], :] for t in range(U) for d in range(2)}
        gbeta = {(t, d): g_s[rows[d][t], :] for t in range(U) for d in range(2)}
        zero_rhs = jnp.zeros((C, 2 * HEAD_D), BF16)
        for (t, d, p), q in zip(pairs, inv):
            r = rows[d][t]
            rhs = []
            for h in (2 * p, 2 * p + 1):
                hs = _head_slice(h)
                gl, bl = 8 * d + h, 8 * d + HEADS + h
                gcol = jnp.broadcast_to(gcs[t, d][:, gl:gl + 1], (C, HEAD_D))
                beta = jnp.broadcast_to(gbeta[t, d][:, bl:bl + 1], (C, HEAD_D))
                kbeta = kn_s[r, hs].astype(F32) * beta
                rhs.append(jnp.concatenate([vn_s[r, hs].astype(F32) * beta, kbeta * jnp.exp(gcol)],
                                           axis=1).astype(BF16))
            rhs_bd = jnp.concatenate([jnp.concatenate([rhs[0], zero_rhs], axis=1),
                                      jnp.concatenate([zero_rhs, rhs[1]], axis=1)], axis=0)
            sol = _dot(q.astype(BF16), rhs_bd)
            for n, h in enumerate((2 * p, 2 * p + 1)):
                hs = _head_slice(h)
                u_s[d, r, hs] = sol[:, 2 * n * HEAD_D:(2 * n + 1) * HEAD_D].astype(u_s.dtype)
                w_s[d, r, hs] = sol[:, (2 * n + 1) * HEAD_D:(2 * n + 2) * HEAD_D].astype(w_s.dtype)

    def recur(grp, final):
        rows = group_rows(grp)
        st = [st_s[n] for n in range(len(chains))]
        for t in range(U):
            gcs = [gc_refs[d][rows[d][t], :] for d in range(2)]
            lhs, kd, egl = [], [], []
            for d, h in chains:
                r, hs = rows[d][t], _head_slice(h)
                gl = 8 * d + h
                gcol = jnp.broadcast_to(gcs[d][:, gl:gl + 1], (C, HEAD_D))
                last = C - 1 if d == 0 else 0
                glast = gcol[last:last + 1, :]
                qe = (qn_s[r, hs].astype(F32) * jnp.exp(gcol)).astype(BF16)
                lhs.append(jnp.concatenate([w_s[d, r, hs], qe], axis=0))
                kd.append((kn_s[r, hs].astype(F32) * jnp.exp(glast - gcol)).astype(BF16))
                egl.append(jnp.exp(glast))
            ws = [_dot(a, s_.astype(BF16)) for a, s_ in zip(lhs, st)]
            yield
            vnew = [(u_s[d, rows[d][t], _head_slice(h)].astype(F32) - x[:C]).astype(BF16)
                    for (d, h), x in zip(chains, ws)]
            outs = [ws[n][C:] + _dot(sc_s[d, h, rows[d][t], :], vnew[n]) for n, (d, h) in enumerate(chains)]
            for n, (d, h) in enumerate(chains):
                r, hs = rows[d][t], _head_slice(h)
                if final:
                    o = outs[n] + o_s[r, hs].astype(F32)
                    on = o * lax.rsqrt(jnp.mean(o * o, axis=-1, keepdims=True) + EPS) * ng_ref[...]
                    y_ref[r, hs] = (on * _silu(z_ref[r, hs].astype(F32))).astype(y_ref.dtype)
                else:
                    o_s[r, hs] = outs[n].astype(o_s.dtype)
            st = [s_ * e + _dot_tn(a, x) for s_, e, a, x in zip(st, egl, kd, vnew)]
            yield
        for n, s_ in enumerate(st):
            st_s[n] = s_

    def alternate(first, second):
        live = [first, second]
        while live:
            for g in list(live):
                if next(g, "done") == "done":
                    live.remove(g)

    def drain(g):
        for _ in g:
            pass

    half = n_groups // 2
    st_s[...] = jnp.zeros_like(st_s)
    prep(0)
    drain(solve(0))
    prep(1)

    def pipelined(final, grp, carry):
        alternate(solve(grp - 1), recur(grp - 2, final))
        prep(grp)
        return carry

    lax.fori_loop(2, half + 2, functools.partial(pipelined, False), 0)
    lax.fori_loop(half + 2, n_groups, functools.partial(pipelined, True), 0)
    alternate(solve(n_groups - 1), recur(n_groups - 2, True))
    drain(recur(n_groups - 1, True))


def _gdn(proj, g_all, gcf, gcb, ng, seq):
    batch = proj.shape[0] // seq
    n_groups = seq // (GDN_CHUNK * PHASE_A_CHUNKS)
    assert seq % (GDN_CHUNK * PHASE_A_CHUNKS) == 0 and n_groups >= 4 and n_groups % 2 == 0, (
        "the 3-stage group pipeline needs an even number (>= 4) of chunk groups per sequence")
    col = lambda c: pl.BlockSpec((seq, BRANCH_W), lambda b, c=c: (b, c))
    lanes = pl.BlockSpec((seq, LANES), lambda b: (b, 0))
    full = lambda shape: pl.BlockSpec(shape, lambda b: (0,) * len(shape))
    per_chain = pltpu.VMEM((2, HEADS, seq, GDN_CHUNK), BF16)
    per_pair = pltpu.VMEM((2, HEADS // 2, seq, 2 * GDN_CHUNK), BF16)
    return pl.pallas_call(
        _gdn_body,
        grid=(batch,),
        in_specs=[col(COL_DQ), col(COL_DK), col(COL_DV), col(COL_DZ), lanes, lanes, lanes, full((1, HEAD_D))],
        out_specs=pl.BlockSpec((seq, BRANCH_W), lambda b: (b, 0)),
        out_shape=jax.ShapeDtypeStruct((batch * seq, BRANCH_W), BF16),
        scratch_shapes=[pltpu.VMEM((2, seq, BRANCH_W), BF16), pltpu.VMEM((2, seq, BRANCH_W), BF16),
                        per_chain, per_pair, pltpu.VMEM((seq, BRANCH_W), BF16),
                        pltpu.VMEM((2 * HEADS, HEAD_D, HEAD_D), F32)],
        compiler_params=_params("arbitrary"),
        name="gdn",
    )(proj, proj, proj, proj, g_all, gcf, gcb, ng)


def _merge_body(final, x_ref, g_ref, ya_ref, yc_ref, qr_ref, kr_ref, rv_ref, rz_ref, st_ref, dec_ref, rng_ref,
                wg_ref, wbo_ref, wo_ref, *rest):
    if final:
        fg_ref, o_ref, yb_s = rest
    else:
        o_ref, yb_s = rest
    C = RET_CHUNK
    items = [(t, h) for t in range(x_ref.shape[0] // C) for h in range(HEADS)]
    rows = lambda t: slice(t * C, (t + 1) * C)
    qr = [qr_ref[rows(t), _head_slice(h)] for t, h in items]
    sc = [_dot_nt(q, kr_ref[rows(t), _head_slice(h)]) * dec_ref[h, 0] for (t, h), q in zip(items, qr)]
    lhs, rhs = [], []
    for n, (t, h) in enumerate(items):
        q = qr[n].astype(F32)
        lhs.append(jnp.concatenate([sc[n], q * dec_ref[h, 1], q * dec_ref[h, 2]], axis=1).astype(BF16))
        rhs.append(jnp.concatenate([rv_ref[rows(t), _head_slice(h)], st_ref[h, t]], axis=0))
    outs = [_dot(a, b) for a, b in zip(lhs, rhs)]
    mus = [jnp.mean(o, axis=-1, keepdims=True) for o in outs]
    ocs = [o - mu for o, mu in zip(outs, mus)]
    var = [jnp.mean(oc * oc, axis=-1, keepdims=True) for oc in ocs]
    for n, (t, h) in enumerate(items):
        hs = _head_slice(h)
        on = ocs[n] * lax.rsqrt(var[n] + EPS) * rng_ref[:, hs]
        yb_s[rows(t), hs] = (on * _silu(rz_ref[rows(t), hs].astype(F32))).astype(yb_s.dtype)

    x = x_ref[...]
    h = (x * lax.rsqrt(jnp.mean(x * x, axis=-1, keepdims=True) + EPS) * g_ref[...]).astype(BF16)
    merged = None
    for n, y_ref in enumerate((ya_ref, yb_s, yc_ref)):
        gate = jax.nn.sigmoid(_dot(h, wg_ref[n]))
        term = gate * _dot(y_ref[...], wbo_ref[n])
        merged = term if merged is None else merged + term
    out = x + _dot(merged.astype(BF16), wo_ref[...])
    if final:
        out = out * lax.rsqrt(jnp.mean(out * out, axis=-1, keepdims=True) + EPS) * fg_ref[...]
    o_ref[...] = out


def _merge(x, g, ya, yc, proj, qr, kr, st, dec, ret_ng, wg, wbo, wo, layer, seq, final_g=None, tm=512):
    tokens = x.shape[0]
    final = final_g is not None
    assert seq % tm == 0 and tm % RET_CHUNK == 0
    tiles_per_seq = seq // tm
    chunks = tm // RET_CHUNK
    row = lambda w: pl.BlockSpec((tm, w), lambda i: (i, 0))
    col = lambda c: pl.BlockSpec((tm, BRANCH_W), lambda i, c=c: (i, c))
    full = lambda shape: pl.BlockSpec(shape, lambda i: (0,) * len(shape))
    of_layer = lambda shape: pl.BlockSpec((None,) + shape, lambda i: (layer,) + (0,) * len(shape))
    states = pl.BlockSpec((None, HEADS, chunks, 2 * HEAD_D, HEAD_D),
                          lambda i: (i // tiles_per_seq, 0, i % tiles_per_seq, 0, 0))
    in_specs = [row(D_MODEL), full((1, D_MODEL)), row(BRANCH_W), row(BRANCH_W),
                row(BRANCH_W), row(BRANCH_W), col(COL_RV), col(COL_RZ), states,
                full((HEADS, 3, RET_CHUNK, RET_CHUNK)), full((1, BRANCH_W)),
                of_layer((N_BRANCH, D_MODEL, D_MODEL)), of_layer((N_BRANCH, BRANCH_W, D_MODEL)),
                of_layer((D_MODEL, D_MODEL))]
    args = [x, g, ya, yc, qr, kr, proj, proj, st, dec, ret_ng, wg, wbo, wo]
    if final:
        in_specs.append(full((1, D_MODEL)))
        args.append(final_g)
    return pl.pallas_call(
        functools.partial(_merge_body, final),
        grid=(tokens // tm,),
        in_specs=in_specs,
        out_specs=row(D_MODEL),
        out_shape=jax.ShapeDtypeStruct((tokens, D_MODEL), F32),
        scratch_shapes=[pltpu.VMEM((tm, BRANCH_W), BF16)],
        compiler_params=_params("arbitrary"),
        name="merge_final" if final else "merge",
    )(*args)


def _rope_tables(seq):
    inv = ROPE_BASE ** (-jnp.arange(0, HEAD_D, 2, dtype=F32) / HEAD_D)
    ang = jnp.arange(seq, dtype=F32)[:, None] * inv[None, :]
    cos = jnp.cos(ang)
    sin = jnp.sin(ang)
    return jnp.concatenate([cos, cos], axis=-1), jnp.concatenate([-sin, sin], axis=-1)


def _lane_row(p):
    row = jnp.zeros((LANES,), F32)
    row = row.at[0:HEADS].set(p[0]).at[2 * HEADS:3 * HEADS].set(p[1])
    return row[None, :]


def kernel(x, norm_g, w_in, gm_ln_g, gm_ln_b, gm_w_s, gm_b_s, ret_decay_logit, ret_norm_g, gdn_conv_w, gdn_a_log, gdn_dt_bias, gdn_norm_g, w_gate, w_branch_out, w_out, final_norm_g):
    batch, seq, _ = x.shape
    xf = x.reshape(batch * seq, D_MODEL)
    cos, sin = _rope_tables(seq)
    w_main = w_in.astype(BF16)
    w_ab = jnp.pad(w_main[:, :, MAIN_COLS:], ((0, 0), (0, 0), (0, LANES - 4 * HEADS)))
    wg = jnp.transpose(w_gate, (0, 2, 1, 3)).astype(BF16)
    wbo = w_branch_out.astype(BF16)
    wo = w_out.astype(BF16)
    for l in range(DEPTH):
        g_row = norm_g[l][None, :]
        bias = jnp.repeat(gm_b_s[l].T, GM_CHUNK, axis=1)
        proj, g_all, gcf, gcb, ya = _in_proj(
            xf, g_row, w_main, w_ab, l, gm_ln_g[l][None, :], gm_ln_b[l][None, :], gm_w_s[l].astype(BF16), bias,
            gdn_conv_w[l], _lane_row(gdn_a_log[l]), _lane_row(gdn_dt_bias[l]), seq)

        dl = jnp.broadcast_to(ret_decay_logit[l][:, :, None, None], (2, HEADS, 8, LANES))
        qr, kr, st, dec = _retention_states(proj, cos, sin, dl)

        yc = _gdn(proj, g_all, gcf, gcb, gdn_norm_g[l][None, :], seq)

        xf = _merge(xf, g_row, ya, yc, proj, qr, kr, st, dec, ret_norm_g[l][None, :], wg, wbo, wo, l, seq,
                    final_norm_g[None, :] if l == DEPTH - 1 else None)
    return xf.reshape(batch, seq, D_MODEL)
```
